```python
import math
import jax, jax.numpy as jnp
from jax import lax
import numpy as np

D_MODEL = 1024
BATCH = 16
SEQ = 256
DEPTH = 4
DEC_BATCH = 8
DEC_SEQ = 4096
PAST_LEN = 512

GRID_W = 64
N_MIXERS = 3
N_LAYERS_A = (DEPTH + 2) // 3
N_LAYERS_B = (DEPTH + 1) // 3
N_LAYERS_C = DEPTH // 3
D_FF = 4 * D_MODEL
N_MOD = 6
RMS_EPS = 1e-6
ROPE_BASE = 10000.0
NEG_INF = -1e30

ATT_HEADS = 16
ATT_KV_HEADS = 4
ATT_HEAD_DIM = 64
ATT_GROUPS = ATT_HEADS // ATT_KV_HEADS
ATT_QKV_W = (ATT_HEADS + 2 * ATT_KV_HEADS) * ATT_HEAD_DIM
WINDOW = 128
BLOCK = 128

RET_HEADS = 4
RET_DK = 256
RET_DV = 512
RET_CHUNK = 128
RET_IN_W = 2 * RET_HEADS * RET_DK + 2 * RET_HEADS * RET_DV

D_RNN = 1024
LRU_BLOCKS = 8
LRU_BW = D_RNN // LRU_BLOCKS
CONV_W = 4
LRU_C = 8.0

kernel_name = "hybrid_flow_trunk_step"


def rms_norm(x, g):
    xf = x.astype(jnp.float32)
    y = xf * lax.rsqrt(jnp.mean(xf * xf, axis=-1, keepdims=True) + RMS_EPS)
    return (y * g.astype(jnp.float32)).astype(x.dtype)


def modulation(cvec, w_ada, b_ada):
    m = jax.nn.silu(cvec) @ w_ada + b_ada
    return jnp.split(m[:, None, :], N_MOD, axis=-1)


def modulate(x, g, shift, scale):
    return rms_norm(x, g) * (1.0 + scale) + shift


def grid_positions(T):
    rows = T // GRID_W
    row = jnp.repeat(jnp.arange(rows, dtype=jnp.int32), GRID_W)
    col = jnp.tile(jnp.arange(GRID_W, dtype=jnp.int32), rows)
    return row, col


def rope_1d(x, pos):
    half = x.shape[-1] // 2
    inv = ROPE_BASE ** (-jnp.arange(half, dtype=jnp.float32) / half)
    ang = pos.astype(jnp.float32)[:, None] * inv[None, :]
    cos = jnp.cos(ang)[:, None, :]
    sin = jnp.sin(ang)[:, None, :]
    xf = x.astype(jnp.float32)
    x1, x2 = xf[..., :half], xf[..., half:]
    return jnp.concatenate([x1 * cos - x2 * sin, x2 * cos + x1 * sin], axis=-1).astype(x.dtype)


def axial_rope(x):
    row, col = grid_positions(x.shape[1])
    h = x.shape[-1] // 2
    return jnp.concatenate([rope_1d(x[..., :h], row), rope_1d(x[..., h:], col)], axis=-1)


def attn_project(h, w_in, q_gain, k_gain):
    B, T, _ = h.shape
    qkv = h @ w_in
    q_end = ATT_HEADS * ATT_HEAD_DIM
    k_end = q_end + ATT_KV_HEADS * ATT_HEAD_DIM
    q = rms_norm(qkv[..., :q_end].reshape(B, T, ATT_HEADS, ATT_HEAD_DIM), q_gain)
    k = rms_norm(qkv[..., q_end:k_end].reshape(B, T, ATT_KV_HEADS, ATT_HEAD_DIM), k_gain)
    v = qkv[..., k_end:].reshape(B, T, ATT_KV_HEADS, ATT_HEAD_DIM)
    return q, k, v


def sink_column(sink, B, Q):
    s = sink.astype(jnp.float32).reshape(ATT_KV_HEADS, ATT_GROUPS)[None, :, :, None, None]
    return jnp.broadcast_to(s, (B, ATT_KV_HEADS, ATT_GROUPS, Q, 1))


def context_attention(q, k, v, sink):
    B, L, H, d = q.shape
    qg = q.reshape(B, L, ATT_KV_HEADS, ATT_GROUPS, d)
    s = jnp.einsum('bqhgd,bkhd->bhgqk', qg, k, preferred_element_type=jnp.float32) * (d ** -0.5)
    p = jax.nn.softmax(jnp.concatenate([s, sink_column(sink, B, L)], axis=-1), axis=-1)[..., :L]
    o = jnp.einsum('bhgqk,bkhd->bqhgd', p.astype(v.dtype), v)
    return o.reshape(B, L, H, d)


def latent_window_attention(q, k, v, k_ctx, v_ctx, sink):
    B, T, H, d = q.shape
    L = k_ctx.shape[1]
    nb = T // BLOCK
    span = BLOCK + 2 * WINDOW
    scale = d ** -0.5
    qg = q.reshape(B, T, ATT_KV_HEADS, ATT_GROUPS, d)
    pad = ((0, 0), (WINDOW, WINDOW), (0, 0), (0, 0))
    kp = jnp.pad(k, pad)
    vp = jnp.pad(v, pad)
    rel = jnp.arange(span)[None, :] - WINDOW - jnp.arange(BLOCK)[:, None]
    near = jnp.abs(rel) <= WINDOW
    sink_col = sink_column(sink, B, BLOCK)

    def one_block(n):
        start = n * BLOCK
        qb = lax.dynamic_slice_in_dim(qg, start, BLOCK, axis=1)
        kb = lax.dynamic_slice_in_dim(kp, start, span, axis=1)
        vb = lax.dynamic_slice_in_dim(vp, start, span, axis=1)
        kpos = start - WINDOW + jnp.arange(span)
        valid = near & ((kpos >= 0) & (kpos < T))[None, :]
        s_lat = jnp.einsum('bqhgd,bkhd->bhgqk', qb, kb, preferred_element_type=jnp.float32) * scale
        s_lat = jnp.where(valid, s_lat, NEG_INF)
        s_ctx = jnp.einsum('bqhgd,bkhd->bhgqk', qb, k_ctx, preferred_element_type=jnp.float32) * scale
        p = jax.nn.softmax(jnp.concatenate([s_lat, s_ctx, sink_col], axis=-1), axis=-1).astype(v.dtype)
        o = (jnp.einsum('bhgqk,bkhd->bqhgd', p[..., :span], vb)
             + jnp.einsum('bhgqk,bkhd->bqhgd', p[..., span:span + L], v_ctx))
        return o.reshape(B, BLOCK, H, d)

    out = lax.map(one_block, jnp.arange(nb))
    return jnp.moveaxis(out, 0, 1).reshape(B, T, H, d)


def retention_scan(q, k, v, log_gamma, s0, inclusive):
    B, T, H, dk = q.shape
    dv = v.shape[-1]
    C = RET_CHUNK
    nc = T // C
    qc = q.reshape(B, nc, C, H, dk)
    kc = k.reshape(B, nc, C, H, dk)
    vc = v.reshape(B, nc, C, H, dv)
    lg = log_gamma.astype(jnp.float32)
    idx = jnp.arange(C, dtype=jnp.float32)
    diff = idx[:, None] - idx[None, :]
    mask = (diff >= 0) if inclusive else (diff > 0)
    decay = jnp.where(mask[None], jnp.exp(lg[:, None, None] * jnp.maximum(diff, 0.0)[None]), 0.0)
    scores = jnp.einsum('bnihd,bnjhd->bnhij', qc, kc, preferred_element_type=jnp.float32) * decay
    o_intra = jnp.einsum('bnhij,bnjhe->bnihe', scores, vc)
    w_state = jnp.exp(lg[:, None] * (C - 1.0 - idx)[None, :])
    w_query = jnp.exp(lg[:, None] * (idx + 1.0)[None, :])
    gamma_chunk = jnp.exp(lg * C)[None, :, None, None]

    def step(s, blk):
        qn, kn, vn = blk
        o_cross = jnp.einsum('bihd,hi,bhde->bihe', qn, w_query, s)
        kv = jnp.einsum('bjhd,hj,bjhe->bhde', kn, w_state, vn)
        return gamma_chunk * s + kv, o_cross

    s_fin, o_cross = lax.scan(step, s0.astype(jnp.float32),
                              (jnp.moveaxis(qc, 1, 0), jnp.moveaxis(kc, 1, 0), jnp.moveaxis(vc, 1, 0)))
    o = o_intra + jnp.moveaxis(o_cross, 0, 1)
    return o.reshape(B, T, H, dv), s_fin


def retention_mixer(h, w_in, w_out, gn_gain, log_decay, s0_f, s0_b, rotate):
    B, T, _ = h.shape
    proj = h @ w_in
    e1 = RET_HEADS * RET_DK
    e2 = 2 * e1
    e3 = e2 + RET_HEADS * RET_DV
    q = proj[..., :e1].reshape(B, T, RET_HEADS, RET_DK)
    k = proj[..., e1:e2].reshape(B, T, RET_HEADS, RET_DK)
    v = proj[..., e2:e3].reshape(B, T, RET_HEADS, RET_DV)
    g = proj[..., e3:]
    if rotate:
        q, k = axial_rope(q), axial_rope(k)
    k = k * (RET_DK ** -0.5)
    o_f, s_f = retention_scan(q, k, v, log_decay[0], s0_f, True)
    o_b, s_b = retention_scan(jnp.flip(q, 1), jnp.flip(k, 1), jnp.flip(v, 1), log_decay[1], s0_b, False)
    o = o_f + jnp.flip(o_b, 1)
    mu = jnp.mean(o, axis=-1, keepdims=True)
    var = jnp.mean(jnp.square(o - mu), axis=-1, keepdims=True)
    o = ((o - mu) * lax.rsqrt(var + RMS_EPS)).reshape(B, T, RET_HEADS * RET_DV) * gn_gain.astype(jnp.float32)
    y = (jax.nn.silu(g) * o.astype(h.dtype)) @ w_out
    return y, s_f, s_b


def centred_depthwise_conv(x, w, b):
    left = CONV_W // 2
    right = CONV_W - 1 - left
    y = lax.conv_general_dilated(x, w[:, None, :], window_strides=(1,), padding=[(left, right)],
                                 dimension_numbers=('NWC', 'WIO', 'NWC'), feature_group_count=x.shape[-1])
    return y + b


def _linear_combine(left, right):
    a_l, u_l = left
    a_r, u_r = right
    return a_l * a_r, a_r * u_l + u_r


def rglru_scan(xc, w_r, b_r, w_i, b_i, lam, h0):
    B, T, _ = xc.shape
    xf = xc.astype(jnp.float32)
    xb = xf.reshape(B, T, LRU_BLOCKS, LRU_BW)
    r = jax.nn.sigmoid(jnp.einsum('btnc,ncd->btnd', xb, w_r.astype(jnp.float32)).reshape(B, T, D_RNN) + b_r)
    i = jax.nn.sigmoid(jnp.einsum('btnc,ncd->btnd', xb, w_i.astype(jnp.float32)).reshape(B, T, D_RNN) + b_i)
    log_a = -LRU_C * r * jax.nn.softplus(-lam.astype(jnp.float32))
    a = jnp.exp(log_a)
    u = jnp.sqrt(-jnp.expm1(2.0 * log_a)) * (i * xf)
    u = u.at[:, 0].add(a[:, 0] * h0.astype(jnp.float32))
    _, hs = lax.associative_scan(_linear_combine, (a, u), axis=1)
    return hs, hs[:, -1]


def lru_mixer(h, w_in, conv_w, conv_b, w_r, b_r, w_i, b_i, lam, w_out, h0_f, h0_b):
    proj = h @ w_in
    gate, xr = proj[..., :D_RNN], proj[..., D_RNN:]
    xc = centred_depthwise_conv(xr, conv_w, conv_b)
    h_f, s_f = rglru_scan(xc, w_r[0], b_r[0], w_i[0], b_i[0], lam[0], h0_f)
    h_b, s_b = rglru_scan(jnp.flip(xc, 1), w_r[1], b_r[1], w_i[1], b_i[1], lam[1], h0_b)
    rec = (h_f + jnp.flip(h_b, 1)).astype(h.dtype)
    y = (jax.nn.gelu(gate) * rec) @ w_out
    return y, s_f, s_b


def sq_relu_mlp(h, w_up, w_down):
    return jnp.square(jax.nn.relu(h @ w_up)) @ w_down


def setup_inputs(seed: int = 0) -> dict:
    key = jax.random.key(seed)
    ks = iter(jax.random.split(key, 48))
    D = D_MODEL

    def nrm(shape, scale):
        return jax.random.normal(next(ks), shape, jnp.float32) * scale

    ret_base = jnp.log1p(-(2.0 ** (-5.0 - jnp.arange(RET_HEADS, dtype=jnp.float32))))
    u = jax.random.uniform(next(ks), (N_LAYERS_C, 2, D_RNN), jnp.float32, 0.9, 0.999)
    return {
        'x_prompt': nrm((BATCH, SEQ, D), 1.0),
        'x_sample': nrm((DEC_BATCH, DEC_SEQ, D), 1.0),
        'cache_attn_k': nrm((DEC_BATCH, N_LAYERS_A, PAST_LEN, ATT_KV_HEADS, ATT_HEAD_DIM), 1.0),
        'cache_attn_v': nrm((DEC_BATCH, N_LAYERS_A, PAST_LEN, ATT_KV_HEADS, ATT_HEAD_DIM), 1.0),
        'state_ret': nrm((DEC_BATCH, N_LAYERS_B, 2, RET_HEADS, RET_DK, RET_DV), 0.5),
        'state_lru': nrm((DEC_BATCH, N_LAYERS_C, 2, D_RNN), 0.5),
        'c': nrm((DEC_BATCH, D), 1.0),
        'c_ctx': nrm((D,), 1.0),
        'norm_mix': 1.0 + nrm((DEPTH, D), 0.02),
        'norm_mlp': 1.0 + nrm((DEPTH, D), 0.02),
        'w_ada': nrm((DEPTH, D, N_MOD * D), 0.5 * D ** -0.5),
        'b_ada': nrm((DEPTH, N_MOD * D), 0.02),
        'w_up': nrm((DEPTH, D, D_FF), D ** -0.5),
        'w_down': nrm((DEPTH, D_FF, D), D_FF ** -0.5),
        'attn_w_in': nrm((N_LAYERS_A, D, ATT_QKV_W), D ** -0.5),
        'attn_w_out': nrm((N_LAYERS_A, ATT_HEADS * ATT_HEAD_DIM, D), (ATT_HEADS * ATT_HEAD_DIM) ** -0.5),
        'attn_q_gain': 1.0 + nrm((N_LAYERS_A, ATT_HEAD_DIM), 0.02),
        'attn_k_gain': 1.0 + nrm((N_LAYERS_A, ATT_HEAD_DIM), 0.02),
        'attn_sink': nrm((N_LAYERS_A, ATT_HEADS), 0.5),
        'ret_w_in': nrm((N_LAYERS_B, D, RET_IN_W), D ** -0.5),
        'ret_w_out': nrm((N_LAYERS_B, RET_HEADS * RET_DV, D), (RET_HEADS * RET_DV) ** -0.5),
        'ret_gn_gain': 1.0 + nrm((N_LAYERS_B, RET_HEADS * RET_DV), 0.02),
        'ret_log_decay': ret_base[None, None, :] * jnp.exp(nrm((N_LAYERS_B, 2, RET_HEADS), 0.05)),
        'lru_w_in': nrm((N_LAYERS_C, D, 2 * D_RNN), D ** -0.5),
        'lru_conv_w': nrm((N_LAYERS_C, CONV_W, D_RNN), CONV_W ** -0.5),
        'lru_conv_b': nrm((N_LAYERS_C, D_RNN), 0.02),
        'lru_w_r': nrm((N_LAYERS_C, 2, LRU_BLOCKS, LRU_BW, LRU_BW), LRU_BW ** -0.5),
        'lru_b_r': nrm((N_LAYERS_C, 2, D_RNN), 0.02),
        'lru_w_i': nrm((N_LAYERS_C, 2, LRU_BLOCKS, LRU_BW, LRU_BW), LRU_BW ** -0.5),
        'lru_b_i': nrm((N_LAYERS_C, 2, D_RNN), 0.02),
        'lru_lambda': jnp.log(u) - jnp.log1p(-u),
        'lru_w_out': nrm((N_LAYERS_C, D_RNN, D), D_RNN ** -0.5),
    }


def reference(x_prompt, x_sample, cache_attn_k, cache_attn_v, state_ret, state_lru, c, c_ctx,
              norm_mix, norm_mlp, w_ada, b_ada, w_up, w_down,
              attn_w_in, attn_w_out, attn_q_gain, attn_k_gain, attn_sink,
              ret_w_in, ret_w_out, ret_gn_gain, ret_log_decay,
              lru_w_in, lru_conv_w, lru_conv_b, lru_w_r, lru_b_r, lru_w_i, lru_b_i, lru_lambda, lru_w_out):
    Bp, Lp, _ = x_prompt.shape
    Bs, Ts, _ = x_sample.shape
    c_prompt = jnp.broadcast_to(c_ctx[None, :], (Bp, D_MODEL))
    xp, xs = x_prompt, x_sample
    new_k, new_v, new_ret, new_lru = [], [], [], []
    for layer in range(DEPTH):
        kind = layer % N_MIXERS
        slot = layer // N_MIXERS
        sh_ap, sc_ap, g_ap, sh_mp, sc_mp, g_mp = modulation(c_prompt, w_ada[layer], b_ada[layer])
        sh_as, sc_as, g_as, sh_ms, sc_ms, g_ms = modulation(c, w_ada[layer], b_ada[layer])
        hp = modulate(xp, norm_mix[layer], sh_ap, sc_ap)
        hs = modulate(xs, norm_mix[layer], sh_as, sc_as)
        if kind == 0:
            q, k, v = attn_project(hp, attn_w_in[slot], attn_q_gain[slot], attn_k_gain[slot])
            yp = context_attention(q, k, v, attn_sink[slot]).reshape(Bp, Lp, -1) @ attn_w_out[slot]
            new_k.append(k)
            new_v.append(v)
            q, k, v = attn_project(hs, attn_w_in[slot], attn_q_gain[slot], attn_k_gain[slot])
            q, k = axial_rope(q), axial_rope(k)
            ys = latent_window_attention(q, k, v, cache_attn_k[:, slot], cache_attn_v[:, slot],
                                         attn_sink[slot]).reshape(Bs, Ts, -1) @ attn_w_out[slot]
        elif kind == 1:
            zero = jnp.zeros((Bp, RET_HEADS, RET_DK, RET_DV), jnp.float32)
            yp, s_f, s_b = retention_mixer(hp, ret_w_in[slot], ret_w_out[slot], ret_gn_gain[slot],
                                           ret_log_decay[slot], zero, zero, False)
            new_ret.append(jnp.stack([s_f, s_b], axis=1).astype(x_prompt.dtype))
            ys, _, _ = retention_mixer(hs, ret_w_in[slot], ret_w_out[slot], ret_gn_gain[slot],
                                       ret_log_decay[slot], state_ret[:, slot, 0], state_ret[:, slot, 1], True)
        else:
            zero = jnp.zeros((Bp, D_RNN), jnp.float32)
            yp, s_f, s_b = lru_mixer(hp, lru_w_in[slot], lru_conv_w[slot], lru_conv_b[slot], lru_w_r[slot],
                                     lru_b_r[slot], lru_w_i[slot], lru_b_i[slot], lru_lambda[slot],
                                     lru_w_out[slot], zero, zero)
            new_lru.append(jnp.stack([s_f, s_b], axis=1).astype(x_prompt.dtype))
            ys, _, _ = lru_mixer(hs, lru_w_in[slot], lru_conv_w[slot], lru_conv_b[slot], lru_w_r[slot],
                                 lru_b_r[slot], lru_w_i[slot], lru_b_i[slot], lru_lambda[slot],
                                 lru_w_out[slot], state_lru[:, slot, 0], state_lru[:, slot, 1])
        xp = xp + g_ap * yp
        xs = xs + g_as * ys
        xp = xp + g_mp * sq_relu_mlp(modulate(xp, norm_mlp[layer], sh_mp, sc_mp), w_up[layer], w_down[layer])
        xs = xs + g_ms * sq_relu_mlp(modulate(xs, norm_mlp[layer], sh_ms, sc_ms), w_up[layer], w_down[layer])
    new_attn_k = jnp.stack(new_k, axis=1)
    new_attn_v = jnp.stack(new_v, axis=1)
    new_state_ret = jnp.stack(new_ret, axis=1)
    new_state_lru = jnp.stack(new_lru, axis=1)
    return (xp, xs, new_attn_k, new_attn_v, new_state_ret, new_state_lru)
```

```python
import functools

import jax
import jax.numpy as jnp
import numpy as np
from jax import lax
from jax.experimental import pallas as pl
from jax.experimental.pallas import tpu as pltpu

F32 = jnp.float32
BF16 = jnp.bfloat16

N_MOD = 6
RMS_EPS = 1e-6
ROPE_BASE = 10000.0
NEG_INF = -1e30
GRID_W = 64
ATT_HEADS = 16
ATT_KV_HEADS = 4
ATT_HEAD_DIM = 64
ATT_GROUPS = ATT_HEADS // ATT_KV_HEADS
ATT_BLOCK = 128
RET_HEADS = 4
RET_DK = 256
RET_DV = 512
RET_CHUNK = 128
LRU_BLOCKS = 8
CONV_W = 4
LRU_C = 8.0

V7X_LANES = 128
V7X_SUBLANES = 8
V7X_VMEM_BYTES = 64 * 1024 * 1024
VMEM_LIMIT = V7X_VMEM_BYTES - 8 * 1024 * 1024

TOKEN_TILE = 512
MOD_ROWS = 16
FF_CHUNK = 1024


def _cparams(sem):
    return pltpu.CompilerParams(dimension_semantics=sem, vmem_limit_bytes=VMEM_LIMIT)


def _resident(shape):
    zeros = (0,) * len(shape)
    return pl.BlockSpec(shape, lambda *_: zeros, pipeline_mode=pl.Buffered(1))


def _dot(a, b):
    return jnp.dot(a, b, preferred_element_type=F32)


def _dot_nt(a, b):
    return lax.dot_general(a, b, (((1,), (1,)), ((), ())), preferred_element_type=F32)


def _dot_tn(a, b):
    return lax.dot_general(a, b, (((0,), (0,)), ((), ())), preferred_element_type=F32)


def _silu(x):
    return x * jax.nn.sigmoid(x)


def _modulate(x, g, shift, scale):
    ms = jnp.mean(x * x, axis=-1, keepdims=True)
    y = x * lax.rsqrt(ms + RMS_EPS) * g
    return y * (1.0 + scale) + shift


def _split_bf16(x):
    hi = x.astype(BF16)
    lo = (x - hi.astype(F32)).astype(BF16)
    return hi, lo


def _mod_kernel(c_ref, w_ref, b_ref, o_ref):
    s = _silu(c_ref[...]).astype(BF16)
    o_ref[...] = _dot(s, w_ref[...].astype(BF16)) + b_ref[...]


def _modulation(cvec, w_ada, b_ada):
    depth, d, w = w_ada.shape
    tn = 1536
    return pl.pallas_call(
        _mod_kernel,
        grid=(depth, w // tn),
        in_specs=[pl.BlockSpec((MOD_ROWS, d), lambda l, j: (0, 0)),
                  pl.BlockSpec((None, d, tn), lambda l, j: (l, 0, j)),
                  pl.BlockSpec((None, 1, tn), lambda l, j: (l, 0, j))],
        out_specs=pl.BlockSpec((None, MOD_ROWS, tn), lambda l, j: (l, 0, j)),
        out_shape=jax.ShapeDtypeStruct((depth, MOD_ROWS, w), F32),
        compiler_params=_cparams(("arbitrary", "arbitrary")),
        name="modulation",
    )(cvec, w_ada, b_ada.reshape(depth, 1, w))


def _rope_tables(t_len, sect):
    half = sect // 2
    inv = ROPE_BASE ** (-jnp.arange(half, dtype=F32) / half)
    t = jnp.arange(t_len, dtype=jnp.int32)

    def one(pos):
        ang = pos.astype(F32)[:, None] * inv[None, :]
        c, s = jnp.cos(ang), jnp.sin(ang)
        return jnp.concatenate([c, c], axis=1), jnp.concatenate([-s, s], axis=1)

    cr, sr = one(t // GRID_W)
    cc, sc = one(t % GRID_W)
    cos = jnp.concatenate([cr, cc], axis=1)
    sin = jnp.concatenate([sr, sc], axis=1)
    return (jnp.stack([jnp.ones_like(cos), cos]), jnp.stack([jnp.zeros_like(sin), sin]))


def _dup_halves(x, lane_low):
    r = pltpu.roll(x, 64, 1)
    return jnp.where(lane_low, x, r), jnp.where(lane_low, r, x)


def _attn_in_kernel(x_ref, g_ref, mod_ref, w_ref, gain_ref, cos_ref, sin_ref, hsum_ref, hexp_ref,
                    q_out, k_out, v_out, kf_out, vf_out):
    d = x_ref.shape[1]
    tm = x_ref.shape[0]
    qk_w = (ATT_HEADS + ATT_KV_HEADS) * ATT_HEAD_DIM
    h = _modulate(x_ref[...], g_ref[...], mod_ref[:, 0:d], mod_ref[:, d:2 * d]).astype(BF16)
    qk = _dot(h, w_ref[:, :qk_w])
    v = _dot(h, w_ref[:, qk_w:])
    hi, lo = _split_bf16(qk * qk)
    ssum = _dot(hi, hsum_ref[...]) + _dot(lo, hsum_ref[...])
    r = lax.rsqrt(ssum * (1.0 / ATT_HEAD_DIM) + RMS_EPS)
    r_hi, r_lo = _split_bf16(r)
    qkn = qk * (_dot(r_hi, hexp_ref[...]) + _dot(r_lo, hexp_ref[...])) * gain_ref[...]

    lane = lax.broadcasted_iota(jnp.int32, (tm, V7X_LANES), 1)
    first = (lane % 32) < 16
    low = lane < 64
    cos = cos_ref[...]
    sin = sin_ref[...]
    n_q = ATT_HEADS * ATT_HEAD_DIM // V7X_LANES
    n_k = ATT_KV_HEADS * ATT_HEAD_DIM // V7X_LANES
    for j in range(n_q + n_k):
        xj = qkn[:, V7X_LANES * j:V7X_LANES * (j + 1)]
        rot = jnp.where(first, pltpu.roll(xj, V7X_LANES - 16, 1), pltpu.roll(xj, 16, 1))
        yj = xj * cos + rot * sin
        if j < n_q:
            q_out[:, V7X_LANES * j:V7X_LANES * (j + 1)] = (yj * (ATT_HEAD_DIM ** -0.5)).astype(BF16)
        else:
            jj = j - n_q
            kf_out[:, V7X_LANES * jj:V7X_LANES * (jj + 1)] = yj
            a, b = _dup_halves(yj, low)
            k_out[:, V7X_LANES * (2 * jj):V7X_LANES * (2 * jj + 1)] = a.astype(BF16)
            k_out[:, V7X_LANES * (2 * jj + 1):V7X_LANES * (2 * jj + 2)] = b.astype(BF16)
    vf_out[...] = v
    for jj in range(n_k):
        a, b = _dup_halves(v[:, V7X_LANES * jj:V7X_LANES * (jj + 1)], low)
        v_out[:, V7X_LANES * (2 * jj):V7X_LANES * (2 * jj + 1)] = a.astype(BF16)
        v_out[:, V7X_LANES * (2 * jj + 1):V7X_LANES * (2 * jj + 2)] = b.astype(BF16)


def _ret_in_kernel(x_ref, g_ref, mod_ref, w_ref, cos_ref, sin_ref, q_out, k_out, v_out, g_out):
    d = x_ref.shape[1]
    e1 = RET_HEADS * RET_DK
    e3 = 2 * e1 + RET_HEADS * RET_DV
    h = _modulate(x_ref[...], g_ref[...], mod_ref[:, 0:d], mod_ref[:, d:2 * d]).astype(BF16)
    n_chunks = e1 // V7X_LANES
    for sec, out in ((0, q_out), (1, k_out)):
        p = _dot(h, w_ref[:, sec * e1:(sec + 1) * e1])
        for j in range(n_chunks):
            s = j % 2
            xj = p[:, V7X_LANES * j:V7X_LANES * (j + 1)]
            yj = (xj * cos_ref[:, V7X_LANES * s:V7X_LANES * (s + 1)]
                  + pltpu.roll(xj, 64, 1) * sin_ref[:, V7X_LANES * s:V7X_LANES * (s + 1)])
            if sec == 1:
                yj = yj * (RET_DK ** -0.5)
            out[:, V7X_LANES * j:V7X_LANES * (j + 1)] = yj.astype(BF16)
    v_out[...] = _dot(h, w_ref[:, 2 * e1:e3]).astype(BF16)
    g_out[...] = _dot(h, w_ref[:, e3:]).astype(BF16)


def _lru_in_kernel(x_ref, g_ref, mod_ref, w_ref, gate_out, xr_out):
    d = x_ref.shape[1]
    d_rnn = gate_out.shape[1]
    h = _modulate(x_ref[...], g_ref[...], mod_ref[:, 0:d], mod_ref[:, d:2 * d]).astype(BF16)
    gate_out[...] = _dot(h, w_ref[:, :d_rnn])
    xr_out[...] = _dot(h, w_ref[:, d_rnn:])


def _token_specs(tm, tps, d, layer):
    return [pl.BlockSpec((tm, d), lambda i: (i, 0)),
            pl.BlockSpec((None, 1, d), lambda i: (layer, 0, 0)),
            pl.BlockSpec((None, None, 1, N_MOD * d), lambda i: (layer, i // tps, 0, 0))]


def _rope_specs(tm, tps, width):
    spec = pl.BlockSpec((None, tm, width), lambda i: (jnp.minimum(i // tps, 1), i % tps, 0))
    return [spec, spec]


def _attend(q_ref, key_fns, val_fns, bias, sink_ref, o_ref):
    nq = q_ref.shape[0]
    low = lax.broadcasted_iota(jnp.int32, (nq, V7X_LANES), 1) < 64
    for h in range(ATT_KV_HEADS):
        kall = jnp.concatenate([f(h) for f in key_fns], axis=0)
        vall = jnp.concatenate([f(h) for f in val_fns], axis=0)
        qs = []
        for g in range(ATT_GROUPS):
            a = ATT_GROUPS * h + g
            piece = q_ref[:, V7X_LANES * (a // 2):V7X_LANES * (a // 2 + 1)]
            qs.append(jnp.where(low if a % 2 == 0 else jnp.logical_not(low), piece, jnp.zeros_like(piece)))
        s = _dot_nt(jnp.concatenate(qs, axis=0), kall)
        ps, dens = [], []
        for g in range(ATT_GROUPS):
            sg = s[nq * g:nq * (g + 1)]
            if bias is not None:
                sg = sg + bias
            snk = sink_ref[ATT_GROUPS * h + g]
            m = jnp.maximum(jnp.max(sg, axis=-1, keepdims=True), snk)
            p = jnp.exp(sg - m)
            dens.append(jnp.sum(p, axis=-1, keepdims=True) + jnp.exp(snk - m))
            ps.append(p.astype(BF16))
        o = _dot(jnp.concatenate(ps, axis=0), vall)
        for pair in range(ATT_GROUPS // 2):
            g0, g1 = 2 * pair, 2 * pair + 1
            o0 = o[nq * g0:nq * (g0 + 1)] / dens[g0]
            o1 = o[nq * g1:nq * (g1 + 1)] / dens[g1]
            col = (ATT_GROUPS * h) // 2 + pair
            o_ref[:, V7X_LANES * col:V7X_LANES * (col + 1)] = jnp.where(low, o0, o1).astype(BF16)


def _col(ref):
    return lambda h: ref[:, V7X_LANES * h:V7X_LANES * (h + 1)]


def _attn_prompt_kernel(sink_ref, q_ref, k_ref, v_ref, o_ref):
    _attend(q_ref, [_col(k_ref)], [_col(v_ref)], None, sink_ref, o_ref)


def _attn_sample_kernel(sink_ref, q_ref, kp_ref, kc_ref, kn_ref, vp_ref, vc_ref, vn_ref,
                        kctx_ref, vctx_ref, o_ref):
    n = pl.program_id(1)
    nb = pl.num_programs(1)
    blk = ATT_BLOCK
    row = lax.broadcasted_iota(jnp.int32, (blk, blk), 0)
    col = lax.broadcasted_iota(jnp.int32, (blk, blk), 1)
    bias_prev = jnp.where(jnp.logical_and(col >= row, n > 0), 0.0, NEG_INF).astype(F32)
    bias_next = jnp.where(jnp.logical_and(col <= row, n < nb - 1), 0.0, NEG_INF).astype(F32)
    n_ctx = kctx_ref.shape[0]
    bias = jnp.concatenate([bias_prev, jnp.zeros((blk, blk), F32), bias_next,
                            jnp.zeros((blk, n_ctx), F32)], axis=1)
    low_ctx = lax.broadcasted_iota(jnp.int32, (n_ctx, V7X_LANES), 1) < 64

    def ctx(ref):
        def get(h):
            a, b = _dup_halves(ref[:, V7X_LANES * (h // 2):V7X_LANES * (h // 2 + 1)], low_ctx)
            return (a if h % 2 == 0 else b).astype(BF16)
        return get

    _attend(q_ref, [_col(kp_ref), _col(kc_ref), _col(kn_ref), ctx(kctx_ref)],
            [_col(vp_ref), _col(vc_ref), _col(vn_ref), ctx(vctx_ref)], bias, sink_ref, o_ref)


def _ret_kernel(lg_ref, q_ref, k_ref, v_ref, g_ref, s0_ref, gn_ref, *rest, t_len, emit_state):
    if emit_state:
        o_ref, sfin_ref, of_buf, sf, sb = rest
    else:
        o_ref, of_buf, sf, sb = rest
    c_len = RET_CHUNK
    nc = t_len // c_len
    h = pl.program_id(1)
    lgf = lg_ref[0, h]
    lgb = lg_ref[1, h]
    row = lax.broadcasted_iota(jnp.int32, (c_len, c_len), 0).astype(F32)
    col = lax.broadcasted_iota(jnp.int32, (c_len, c_len), 1).astype(F32)
    diff = row - col
    decay_f = jnp.where(diff >= 0, jnp.exp(lgf * jnp.maximum(diff, 0.0)), 0.0)
    decay_b = jnp.where(diff < 0, jnp.exp(lgb * jnp.maximum(-diff, 0.0)), 0.0)
    idx = lax.broadcasted_iota(jnp.int32, (c_len, 1), 0).astype(F32)
    one = jnp.ones((1, 1), F32)
    wq_f, ws_f, gc_f = jnp.exp(lgf * (idx + 1.0)), jnp.exp(lgf * (c_len - 1.0 - idx)), jnp.exp(lgf * c_len * one)
    wq_b, ws_b, gc_b = jnp.exp(lgb * (c_len - idx)), jnp.exp(lgb * idx), jnp.exp(lgb * c_len * one)
    sf[...] = s0_ref[0]
    sb[...] = s0_ref[1]

    def chunk(c, state, decay, wq, ws, gc):
        sl = pl.ds(pl.multiple_of(c * c_len, c_len), c_len)
        qc, kc, vc = q_ref[sl, :], k_ref[sl, :], v_ref[sl, :]
        s_prev = state[...]
        sc = (_dot_nt(qc, kc) * decay).astype(BF16)
        o = _dot(sc, vc) + _dot(qc, s_prev.astype(BF16)) * wq
        kw = (kc.astype(F32) * ws).astype(BF16)
        state[...] = gc * s_prev + _dot_tn(kw, vc)
        return sl, o

    def fwd(c, carry):
        sl, o = chunk(c, sf, decay_f, wq_f, ws_f, gc_f)
        of_buf[sl, :] = o
        return carry

    lax.fori_loop(0, nc, fwd, 0)

    def bwd(i, carry):
        sl, o = chunk(nc - 1 - i, sb, decay_b, wq_b, ws_b, gc_b)
        o = o + of_buf[sl, :]
        mu = jnp.mean(o, axis=-1, keepdims=True)
        dev = o - mu
        var = jnp.mean(dev * dev, axis=-1, keepdims=True)
        on = dev * lax.rsqrt(var + RMS_EPS) * gn_ref[...]
        o_ref[sl, :] = (_silu(g_ref[sl, :].astype(F32)) * on).astype(BF16)
        return carry

    lax.fori_loop(0, nc, bwd, 0)
    if emit_state:
        sfin_ref[0] = sf[...]
        sfin_ref[1] = sb[...]


def _softplus(x):
    return jnp.maximum(x, 0.0) + jnp.log1p(jnp.exp(-jnp.abs(x)))


def _lru_kernel(xr_ref, prev_ref, next_ref, gate_ref, cw_ref, cb_ref, wg_ref, br_ref, bi_ref, lam_ref,
                h0_ref, y_ref, hfin_ref, hf_buf, a_s, u_s, hc, *, nc):
    tc, d_rnn = xr_ref.shape
    bw = d_rnn // LRU_BLOCKS
    s = pl.program_id(1)
    is_b = s >= nc
    c = jnp.where(is_b, 2 * nc - 1 - s, s)

    x = xr_ref[...]
    rowi = lax.broadcasted_iota(jnp.int32, (tc, d_rnn), 0)
    p1 = jnp.where(c > 0, prev_ref[V7X_SUBLANES - 1:V7X_SUBLANES, :], 0.0)
    p2 = jnp.where(c > 0, prev_ref[V7X_SUBLANES - 2:V7X_SUBLANES - 1, :], 0.0)
    n1 = jnp.where(c < nc - 1, next_ref[0:1, :], 0.0)
    xm1 = jnp.where(rowi == 0, p1, pltpu.roll(x, 1, 0))
    xm2 = jnp.where(rowi == 0, p2, jnp.where(rowi == 1, p1, pltpu.roll(x, 2, 0)))
    xp1 = jnp.where(rowi == tc - 1, n1, pltpu.roll(x, tc - 1, 0))
    xc = (cw_ref[0:1, :] * xm2 + cw_ref[1:2, :] * xm1 + cw_ref[2:3, :] * x + cw_ref[3:4, :] * xp1
          + cb_ref[...])

    xcb = xc.astype(BF16)
    sp = _softplus(-lam_ref[...])
    for n in range(LRU_BLOCKS):
        cs = slice(bw * n, bw * (n + 1))
        z = _dot(xcb[:, cs], wg_ref[n])
        r = jax.nn.sigmoid(z[:, :bw] + br_ref[:, cs])
        i = jax.nn.sigmoid(z[:, bw:] + bi_ref[:, cs])
        log_a = -LRU_C * r * sp[:, cs]
        a = jnp.exp(log_a)
        u = jnp.sqrt(-jnp.tanh(log_a) * (a * a + 1.0)) * (i * xc[:, cs])
        a_s[:, cs] = a
        u_s[:, cs] = u

    @pl.when(s == 0)
    def _():
        hc[...] = h0_ref[0:1, :]

    @pl.when(s == nc)
    def _():
        hc[...] = h0_ref[1:2, :]

    base = pl.multiple_of(c * tc, tc)

    @pl.when(jnp.logical_not(is_b))
    def _():
        def step(t, h):
            h = a_s[pl.ds(t, 1), :] * h + u_s[pl.ds(t, 1), :]
            hf_buf[pl.ds(base + t, 1), :] = h
            return h
        h = lax.fori_loop(0, tc, step, hc[...], unroll=8)
        hc[...] = h

        @pl.when(s == nc - 1)
        def _():
            hfin_ref[0:1, :] = h

    @pl.when(is_b)
    def _():
        def step(k, h):
            t = tc - 1 - k
            h = a_s[pl.ds(t, 1), :] * h + u_s[pl.ds(t, 1), :]
            u_s[pl.ds(t, 1), :] = h
            return h
        h = lax.fori_loop(0, tc, step, hc[...], unroll=8)
        hc[...] = h
        rec = hf_buf[pl.ds(base, tc), :] + u_s[...]
        y_ref[...] = (jax.nn.gelu(gate_ref[...], approximate=True) * rec).astype(BF16)

        @pl.when(s == 2 * nc - 1)
        def _():
            hfin_ref[1:2, :] = h


def _out_mlp_kernel(x_ref, yp_ref, ys_ref, mod_ref, gm_ref, wo_ref, wu_ref, wd_ref, o_ref, *, n_prompt):
    d = x_ref.shape[1]
    d_ff = wu_ref.shape[1]
    i = pl.program_id(0)
    y = jnp.where(i < n_prompt, yp_ref[...], ys_ref[...])
    x1 = x_ref[...] + mod_ref[:, 2 * d:3 * d] * _dot(y, wo_ref[...])
    hn = _modulate(x1, gm_ref[...], mod_ref[:, 3 * d:4 * d], mod_ref[:, 4 * d:5 * d]).astype(BF16)
    acc = jnp.zeros(x1.shape, F32)
    for c in range(d_ff // FF_CHUNK):
        hh = jnp.maximum(_dot(hn, wu_ref[:, FF_CHUNK * c:FF_CHUNK * (c + 1)]), 0.0)
        acc = acc + _dot((hh * hh).astype(BF16), wd_ref[FF_CHUNK * c:FF_CHUNK * (c + 1), :])
    o_ref[...] = x1 + mod_ref[:, 5 * d:6 * d] * acc


class _Dims:
    def __init__(self, bp, lp, bs, ts, d):
        assert bp * lp == ts, "prompt tokens must fill exactly one segment"
        self.bp, self.lp, self.bs, self.ts, self.d = bp, lp, bs, ts, d
        self.nseg = 1 + bs
        self.n = self.nseg * ts
        self.tm = min(TOKEN_TILE, ts)
        assert ts % self.tm == 0 and ts % ATT_BLOCK == 0 and lp % RET_CHUNK == 0
        self.tps = ts // self.tm
        self.ntiles = self.n // self.tm


def _attn_layer(dm, layer, x, norm_mix, mods, w_in, q_gain, k_gain, sink, cache_k, cache_v, rope):
    d, tm, tps, n = dm.d, dm.tm, dm.tps, dm.n
    hd = ATT_HEAD_DIM
    q_w, kv_w = ATT_HEADS * hd, ATT_KV_HEADS * hd
    gain = jnp.concatenate([jnp.tile(q_gain, ATT_HEADS), jnp.tile(k_gain, ATT_KV_HEADS)])[None, :]
    heads = (q_w + kv_w) // hd
    head_of_lane = np.arange(q_w + kv_w) // hd
    hsum = jnp.asarray(head_of_lane[:, None] == np.arange(V7X_LANES)[None, :], BF16)
    hexp = jnp.asarray(np.arange(V7X_LANES)[:, None] == head_of_lane[None, :], BF16)
    assert heads <= V7X_LANES
    cos, sin = rope
    tok = lambda w: pl.BlockSpec((tm, w), lambda i: (i, 0))
    q, k, v, kf, vf = pl.pallas_call(
        _attn_in_kernel,
        grid=(dm.ntiles,),
        in_specs=_token_specs(tm, tps, d, layer) + [
            _resident(w_in.shape), _resident(gain.shape)] + _rope_specs(tm, tps, V7X_LANES) + [
            _resident(hsum.shape), _resident(hexp.shape)],
        out_specs=[tok(q_w), tok(2 * kv_w), tok(2 * kv_w), tok(kv_w), tok(kv_w)],
        out_shape=[jax.ShapeDtypeStruct((n, q_w), BF16), jax.ShapeDtypeStruct((n, 2 * kv_w), BF16),
                   jax.ShapeDtypeStruct((n, 2 * kv_w), BF16), jax.ShapeDtypeStruct((n, kv_w), F32),
                   jax.ShapeDtypeStruct((n, kv_w), F32)],
        compiler_params=_cparams(("arbitrary",)),
        name="attn_in",
    )(x, norm_mix, mods, w_in, gain, cos, sin, hsum, hexp)

    smem = pl.BlockSpec(memory_space=pltpu.SMEM)
    lp, bp, bs, ts = dm.lp, dm.bp, dm.bs, dm.ts
    y_p = pl.pallas_call(
        _attn_prompt_kernel,
        grid=(bp,),
        in_specs=[smem, pl.BlockSpec((lp, q_w), lambda b: (b, 0)),
                  pl.BlockSpec((lp, 2 * kv_w), lambda b: (b, 0)),
                  pl.BlockSpec((lp, 2 * kv_w), lambda b: (b, 0))],
        out_specs=pl.BlockSpec((lp, q_w), lambda b: (b, 0)),
        out_shape=jax.ShapeDtypeStruct((ts, q_w), BF16),
        compiler_params=_cparams(("arbitrary",)),
        name="attn_prompt",
    )(sink, q, k, v)

    blk = ATT_BLOCK
    nb = ts // blk
    cur = lambda b, j: ((1 + b) * nb + j, 0)
    prv = lambda b, j: ((1 + b) * nb + jnp.maximum(j - 1, 0), 0)
    nxt = lambda b, j: ((1 + b) * nb + jnp.minimum(j + 1, nb - 1), 0)
    past = cache_k.shape[1]
    ctx_spec = pl.BlockSpec((None, past, kv_w), lambda b, j: (b, 0, 0))
    kv_spec = lambda f: pl.BlockSpec((blk, 2 * kv_w), f)
    y_s = pl.pallas_call(
        _attn_sample_kernel,
        grid=(bs, nb),
        in_specs=[smem, pl.BlockSpec((blk, q_w), cur), kv_spec(prv), kv_spec(cur), kv_spec(nxt),
                  kv_spec(prv), kv_spec(cur), kv_spec(nxt), ctx_spec, ctx_spec],
        out_specs=pl.BlockSpec((blk, q_w), lambda b, j: (b * nb + j, 0)),
        out_shape=jax.ShapeDtypeStruct((bs * ts, q_w), BF16),
        compiler_params=_cparams(("arbitrary", "arbitrary")),
        name="attn_sample",
    )(sink, q, k, k, k, v, v, v, cache_k, cache_v)
    return y_p, y_s, kf[:ts], vf[:ts]


def _ret_layer(dm, layer, x, norm_mix, mods, w_in, gn_gain, log_decay, state, rope):
    d, tm, tps, n = dm.d, dm.tm, dm.tps, dm.n
    e1, ev = RET_HEADS * RET_DK, RET_HEADS * RET_DV
    cos, sin = rope
    tok = lambda w: pl.BlockSpec((tm, w), lambda i: (i, 0))
    q, k, v, g = pl.pallas_call(
        _ret_in_kernel,
        grid=(dm.ntiles,),
        in_specs=_token_specs(tm, tps, d, layer) + [_resident(w_in.shape)] + _rope_specs(tm, tps, RET_DK),
        out_specs=[tok(e1), tok(e1), tok(ev), tok(ev)],
        out_shape=[jax.ShapeDtypeStruct((n, e1), BF16), jax.ShapeDtypeStruct((n, e1), BF16),
                   jax.ShapeDtypeStruct((n, ev), BF16), jax.ShapeDtypeStruct((n, ev), BF16)],
        compiler_params=_cparams(("arbitrary",)),
        name="ret_in",
    )(x, norm_mix, mods, w_in, cos, sin)

    smem = pl.BlockSpec(memory_space=pltpu.SMEM)
    gn = gn_gain[None, :]
    lp, bp, bs, ts = dm.lp, dm.bp, dm.bs, dm.ts

    def call(t_len, nbatch, seg0, s0, emit_state, name):
        rows = lambda w: pl.BlockSpec((t_len, w), lambda b, h: (seg0 + b, h))
        out_shape = [jax.ShapeDtypeStruct((nbatch * t_len, ev), BF16)]
        out_specs = [pl.BlockSpec((t_len, RET_DV), lambda b, h: (b, h))]
        if emit_state:
            out_shape.append(jax.ShapeDtypeStruct((nbatch, 2, RET_HEADS, RET_DK, RET_DV), F32))
            out_specs.append(pl.BlockSpec((None, 2, None, RET_DK, RET_DV), lambda b, h: (b, 0, h, 0, 0)))
        return pl.pallas_call(
            functools.partial(_ret_kernel, t_len=t_len, emit_state=emit_state),
            grid=(nbatch, RET_HEADS),
            in_specs=[smem, rows(RET_DK), rows(RET_DK), rows(RET_DV), rows(RET_DV),
                      pl.BlockSpec((None, 2, None, RET_DK, RET_DV), lambda b, h: (b, 0, h, 0, 0)),
                      pl.BlockSpec((1, RET_DV), lambda b, h: (0, h))],
            out_specs=out_specs, out_shape=out_shape,
            scratch_shapes=[pltpu.VMEM((t_len, RET_DV), F32), pltpu.VMEM((RET_DK, RET_DV), F32),
                            pltpu.VMEM((RET_DK, RET_DV), F32)],
            compiler_params=_cparams(("arbitrary", "arbitrary")),
            name=name,
        )(log_decay, q, k, v, g, s0, gn)

    zero_state = jnp.zeros((bp, 2, RET_HEADS, RET_DK, RET_DV), F32)
    y_p, s_new = call(lp, bp, 0, zero_state, True, "ret_prompt")
    (y_s,) = call(ts, bs, 1, state, False, "ret_sample")
    return y_p, y_s, s_new


def _lru_layer(dm, layer, x, norm_mix, mods, w_in, conv_w, conv_b, w_r, b_r, w_i, b_i, lam, state):
    d, tm, tps, n = dm.d, dm.tm, dm.tps, dm.n
    d_rnn = w_in.shape[1] // 2
    tok = lambda w: pl.BlockSpec((tm, w), lambda i: (i, 0))
    gate, xr = pl.pallas_call(
        _lru_in_kernel,
        grid=(dm.ntiles,),
        in_specs=_token_specs(tm, tps, d, layer) + [_resident(w_in.shape)],
        out_specs=[tok(d_rnn), tok(d_rnn)],
        out_shape=[jax.ShapeDtypeStruct((n, d_rnn), F32), jax.ShapeDtypeStruct((n, d_rnn), F32)],
        compiler_params=_cparams(("arbitrary",)),
        name="lru_in",
    )(x, norm_mix, mods, w_in)

    wg = jnp.concatenate([w_r, w_i], axis=-1).astype(BF16)
    vec = lambda a: a[:, None, :]
    lp, bp, bs, ts = dm.lp, dm.bp, dm.bs, dm.ts
    sub = V7X_SUBLANES

    def call(t_len, nbatch, row0, h0, name):
        tc = t_len if t_len <= 256 else min(512, t_len // 2)
        nc = t_len // tc
        cb = row0 // tc
        chunk = lambda s: jnp.where(s >= nc, 2 * nc - 1 - s, s)
        late = lambda s: jnp.where(s >= nc, 2 * nc - 1 - s, nc - 1)
        dirv = lambda s: jnp.where(s >= nc, 1, 0)
        r8 = tc // sub
        first8 = lambda b: (row0 + b * t_len) // sub
        prev8 = lambda b, s: (jnp.maximum(first8(b) + chunk(s) * r8 - 1, first8(b)), 0)
        next8 = lambda b, s: (jnp.minimum(first8(b) + (chunk(s) + 1) * r8, first8(b) + t_len // sub - 1), 0)
        dspec = lambda shape: pl.BlockSpec((None,) + shape, lambda b, s: (dirv(s),) + (0,) * len(shape))
        return pl.pallas_call(
            functools.partial(_lru_kernel, nc=nc),
            grid=(nbatch, 2 * nc),
            in_specs=[pl.BlockSpec((tc, d_rnn), lambda b, s: (cb + b * nc + chunk(s), 0)),
                      pl.BlockSpec((sub, d_rnn), prev8), pl.BlockSpec((sub, d_rnn), next8),
                      pl.BlockSpec((tc, d_rnn), lambda b, s: (cb + b * nc + late(s), 0)),
                      pl.BlockSpec((CONV_W, d_rnn), lambda b, s: (0, 0)),
                      pl.BlockSpec((1, d_rnn), lambda b, s: (0, 0)),
                      dspec(wg.shape[1:]), dspec((1, d_rnn)), dspec((1, d_rnn)), dspec((1, d_rnn)),
                      pl.BlockSpec((None, 2, d_rnn), lambda b, s: (b, 0, 0))],
            out_specs=[pl.BlockSpec((tc, d_rnn), lambda b, s: (b * nc + late(s), 0)),
                       pl.BlockSpec((None, 2, d_rnn), lambda b, s: (b, 0, 0))],
            out_shape=[jax.ShapeDtypeStruct((nbatch * t_len, d_rnn), BF16),
                       jax.ShapeDtypeStruct((nbatch, 2, d_rnn), F32)],
            scratch_shapes=[pltpu.VMEM((t_len, d_rnn), F32), pltpu.VMEM((tc, d_rnn), F32),
                            pltpu.VMEM((tc, d_rnn), F32), pltpu.VMEM((1, d_rnn), F32)],
            compiler_params=_cparams(("arbitrary", "arbitrary")),
            name=name,
        )(xr, xr, xr, gate, conv_w, conv_b[None, :], wg, vec(b_r), vec(b_i), vec(lam), h0)

    y_p, h_new = call(lp, bp, 0, jnp.zeros((bp, 2, d_rnn), F32), "lru_prompt")
    y_s, _ = call(ts, bs, ts, state, "lru_sample")
    return y_p, y_s, h_new


def _out_mlp(dm, layer, x, y_p, y_s, norm_mlp, mods, w_out, w_up, w_down):
    d, tm, tps, n = dm.d, dm.tm, dm.tps, dm.n
    din = w_out.shape[0]
    return pl.pallas_call(
        functools.partial(_out_mlp_kernel, n_prompt=tps),
        grid=(dm.ntiles,),
        in_specs=[pl.BlockSpec((tm, d), lambda i: (i, 0)),
                  pl.BlockSpec((tm, din), lambda i: (jnp.minimum(i, tps - 1), 0)),
                  pl.BlockSpec((tm, din), lambda i: (jnp.maximum(i - tps, 0), 0)),
                  pl.BlockSpec((None, None, 1, N_MOD * d), lambda i: (layer, i // tps, 0, 0)),
                  pl.BlockSpec((None, 1, d), lambda i: (layer, 0, 0)),
                  _resident(w_out.shape), _resident(w_up.shape), _resident(w_down.shape)],
        out_specs=pl.BlockSpec((tm, d), lambda i: (i, 0)),
        out_shape=jax.ShapeDtypeStruct((n, d), F32),
        compiler_params=_cparams(("arbitrary",)),
        name="out_mlp",
    )(x, y_p, y_s, mods, norm_mlp, w_out, w_up, w_down)


def kernel(x_prompt, x_sample, cache_attn_k, cache_attn_v, state_ret, state_lru, c, c_ctx, norm_mix, norm_mlp, w_ada, b_ada, w_up, w_down, attn_w_in, attn_w_out, attn_q_gain, attn_k_gain, attn_sink, ret_w_in, ret_w_out, ret_gn_gain, ret_log_decay, lru_w_in, lru_conv_w, lru_conv_b, lru_w_r, lru_b_r, lru_w_i, lru_b_i, lru_lambda, lru_w_out):
    bp, lp, d = x_prompt.shape
    bs, ts, _ = x_sample.shape
    dm = _Dims(bp, lp, bs, ts, d)
    depth = w_ada.shape[0]
    assert dm.nseg <= MOD_ROWS

    x = jnp.concatenate([x_prompt.reshape(ts, d), x_sample.reshape(bs * ts, d)], axis=0)
    cvec = jnp.concatenate([c_ctx[None, :], c, jnp.zeros((MOD_ROWS - dm.nseg, d), F32)], axis=0)
    mods = _modulation(cvec, w_ada, b_ada)[:, :dm.nseg].reshape(depth, dm.nseg, 1, N_MOD * d)
    nmix = norm_mix[:, None, :]
    nmlp = norm_mlp[:, None, :]

    rope_attn = tuple(jnp.tile(t, (1, 1, 2)) for t in _rope_tables(ts, ATT_HEAD_DIM // 2))
    rope_ret = _rope_tables(ts, RET_DK // 2)
    kv_w = ATT_KV_HEADS * ATT_HEAD_DIM
    past = cache_attn_k.shape[2]
    ck = cache_attn_k.reshape(bs, -1, past, kv_w)
    cv = cache_attn_v.reshape(bs, -1, past, kv_w)

    new_k, new_v, new_ret, new_lru = [], [], [], []
    for layer in range(depth):
        kind, slot = layer % 3, layer // 3
        if kind == 0:
            y_p, y_s, kf, vf = _attn_layer(dm, layer, x, nmix, mods, attn_w_in[slot].astype(BF16),
                                           attn_q_gain[slot], attn_k_gain[slot], attn_sink[slot],
                                           ck[:, slot], cv[:, slot], rope_attn)
            new_k.append(kf.reshape(bp, lp, ATT_KV_HEADS, ATT_HEAD_DIM))
            new_v.append(vf.reshape(bp, lp, ATT_KV_HEADS, ATT_HEAD_DIM))
            w_out = attn_w_out[slot]
        elif kind == 1:
            y_p, y_s, s_new = _ret_layer(dm, layer, x, nmix, mods, ret_w_in[slot].astype(BF16),
                                         ret_gn_gain[slot], ret_log_decay[slot], state_ret[:, slot],
                                         rope_ret)
            new_ret.append(s_new)
            w_out = ret_w_out[slot]
        else:
            y_p, y_s, h_new = _lru_layer(dm, layer, x, nmix, mods, lru_w_in[slot].astype(BF16),
                                         lru_conv_w[slot], lru_conv_b[slot], lru_w_r[slot], lru_b_r[slot],
                                         lru_w_i[slot], lru_b_i[slot], lru_lambda[slot], state_lru[:, slot])
            new_lru.append(h_new)
            w_out = lru_w_out[slot]
        x = _out_mlp(dm, layer, x, y_p, y_s, nmlp, mods, w_out.astype(BF16),
                     w_up[layer].astype(BF16), w_down[layer].astype(BF16))

    y_prompt = x[:ts].reshape(bp, lp, d)
    y_sample = x[ts:].reshape(bs, ts, d)
    return (y_prompt, y_sample, jnp.stack(new_k, axis=1), jnp.stack(new_v, axis=1),
            jnp.stack(new_ret, axis=1), jnp.stack(new_lru, axis=1))
```

```python
import functools

import jax
import jax.numpy as jnp
import numpy as np
from jax import lax
from jax.experimental import pallas as pl
from jax.experimental.pallas import tpu as pltpu

F32 = jnp.float32
BF16 = jnp.bfloat16

N_MOD = 6
RMS_EPS = 1e-6
ROPE_BASE = 10000.0
NEG_INF = -1e30
LOG2E = 1.4426950408889634
GRID_W = 64
ATT_HEADS = 16
ATT_KV_HEADS = 4
ATT_HEAD_DIM = 64
ATT_GROUPS = ATT_HEADS // ATT_KV_HEADS
ATT_BLOCK = 128
RET_HEADS = 4
RET_DK = 256
RET_DV = 512
LRU_BLOCKS = 8
CONV_W = 4
LRU_C = 8.0

V7X_LANES = 128
V7X_SUBLANES = 8
V7X_VMEM_BYTES = 64 * 1024 * 1024
VMEM_LIMIT = V7X_VMEM_BYTES - 8 * 1024 * 1024

TOKEN_TILE = 512
MOD_ROWS = 16
FF_CHUNK = 1024
SOFTMAX_ROWS = 16
RET_KERNEL_CHUNK = 256


def _cparams(sem):
    return pltpu.CompilerParams(dimension_semantics=sem, vmem_limit_bytes=VMEM_LIMIT)


def _resident(shape):
    zeros = (0,) * len(shape)
    return pl.BlockSpec(shape, lambda *_: zeros, pipeline_mode=pl.Buffered(1))


def _dot(a, b):
    return jnp.dot(a, b, preferred_element_type=F32)


def _dot_nt(a, b):
    return lax.dot_general(a, b, (((1,), (1,)), ((), ())), preferred_element_type=F32)


def _dot_tn(a, b):
    return lax.dot_general(a, b, (((0,), (0,)), ((), ())), preferred_element_type=F32)


def _silu(x):
    return x * jax.nn.sigmoid(x)


def _modulate(x, g, shift, scale):
    ms = jnp.mean(x * x, axis=-1, keepdims=True)
    y = x * lax.rsqrt(ms + RMS_EPS) * g
    return y * (1.0 + scale) + shift


def _split_bf16(x):
    hi = x.astype(BF16)
    lo = (x - hi.astype(F32)).astype(BF16)
    return hi, lo


def _read_x(refs, n_prompt, split_x):
    if split_x:
        x = jnp.where(pl.program_id(0) < n_prompt, refs[0][...], refs[1][...])
        return x, refs[2:]
    return refs[0][...], refs[1:]


def _mod_kernel(c_ref, w_ref, b_ref, o_ref):
    s = _silu(c_ref[...]).astype(BF16)
    o_ref[...] = _dot(s, w_ref[...].astype(BF16)) + b_ref[...]


def _modulation(cvec, w_ada, b_ada):
    depth, d, w = w_ada.shape
    tn = 1536
    return pl.pallas_call(
        _mod_kernel,
        grid=(depth, w // tn),
        in_specs=[pl.BlockSpec((MOD_ROWS, d), lambda l, j: (0, 0)),
                  pl.BlockSpec((None, d, tn), lambda l, j: (l, 0, j)),
                  pl.BlockSpec((None, 1, tn), lambda l, j: (l, 0, j))],
        out_specs=pl.BlockSpec((None, MOD_ROWS, tn), lambda l, j: (l, 0, j)),
        out_shape=jax.ShapeDtypeStruct((depth, MOD_ROWS, w), F32),
        compiler_params=_cparams(("arbitrary", "arbitrary")),
        name="modulation",
    )(cvec, w_ada, b_ada.reshape(depth, 1, w))


def _rope_tables(t_len, sect):
    half = sect // 2
    inv = ROPE_BASE ** (-jnp.arange(half, dtype=F32) / half)
    t = jnp.arange(t_len, dtype=jnp.int32)

    def one(pos):
        ang = pos.astype(F32)[:, None] * inv[None, :]
        c, s = jnp.cos(ang), jnp.sin(ang)
        return jnp.concatenate([c, c], axis=1), jnp.concatenate([-s, s], axis=1)

    cr, sr = one(t // GRID_W)
    cc, sc = one(t % GRID_W)
    cos = jnp.concatenate([cr, cc], axis=1)
    sin = jnp.concatenate([sr, sc], axis=1)
    return (jnp.stack([jnp.ones_like(cos), cos]), jnp.stack([jnp.zeros_like(sin), sin]))


def _dup_halves(x, lane_low):
    r = pltpu.roll(x, 64, 1)
    return jnp.where(lane_low, x, r), jnp.where(lane_low, r, x)


def _ones_halves(x, lane_low):
    return jnp.where(lane_low, x, 1.0), jnp.where(lane_low, pltpu.roll(x, 64, 1), 1.0)


def _attn_in_kernel(*refs, n_prompt, split_x):
    x, rest = _read_x(refs, n_prompt, split_x)
    (g_ref, mod_ref, w_ref, gain_ref, cos_ref, sin_ref, hsum_ref, hexp_ref,
     q_out, k_out, v_out, kf_out, vf_out) = rest
    tm, d = x.shape
    is_prompt = pl.program_id(0) < n_prompt
    h = _modulate(x, g_ref[...], mod_ref[:, 0:d], mod_ref[:, d:2 * d]).astype(BF16)

    lane = lax.broadcasted_iota(jnp.int32, (tm, V7X_LANES), 1)
    first = (lane % 32) < 16
    low = lane < 64
    cos = cos_ref[...]
    sin = sin_ref[...]
    gw = hsum_ref.shape[0]
    per = gw // V7X_LANES
    n_q = ATT_HEADS * ATT_HEAD_DIM // V7X_LANES
    n_groups = (ATT_HEADS + ATT_KV_HEADS) * ATT_HEAD_DIM // gw
    for grp in range(n_groups):
        p = _dot(h, w_ref[:, gw * grp:gw * (grp + 1)])
        ssum = _dot((p * p).astype(BF16), hsum_ref[...])
        r_hi, r_lo = _split_bf16(lax.rsqrt(ssum * (1.0 / ATT_HEAD_DIM) + RMS_EPS))
        pn = p * (_dot(r_hi, hexp_ref[...]) + _dot(r_lo, hexp_ref[...])) * gain_ref[:, gw * grp:gw * (grp + 1)]
        for jl in range(per):
            j = per * grp + jl
            xj = pn[:, V7X_LANES * jl:V7X_LANES * (jl + 1)]
            rot = jnp.where(first, pltpu.roll(xj, V7X_LANES - 16, 1), pltpu.roll(xj, 16, 1))
            yj = xj * cos + rot * sin
            if j < n_q:
                q_out[:, V7X_LANES * j:V7X_LANES * (j + 1)] = (yj * (ATT_HEAD_DIM ** -0.5 * LOG2E)).astype(BF16)
            else:
                jj = j - n_q

                @pl.when(is_prompt)
                def _(yj=yj, jj=jj):
                    kf_out[:, V7X_LANES * jj:V7X_LANES * (jj + 1)] = yj

                a, b = _dup_halves(yj, low)
                k_out[:, V7X_LANES * (2 * jj):V7X_LANES * (2 * jj + 1)] = a.astype(BF16)
                k_out[:, V7X_LANES * (2 * jj + 1):V7X_LANES * (2 * jj + 2)] = b.astype(BF16)
    v = _dot(h, w_ref[:, gw * n_groups:])

    @pl.when(is_prompt)
    def _():
        vf_out[...] = v

    for jj in range(v.shape[1] // V7X_LANES):
        a, b = _ones_halves(v[:, V7X_LANES * jj:V7X_LANES * (jj + 1)], low)
        v_out[:, V7X_LANES * (2 * jj):V7X_LANES * (2 * jj + 1)] = a.astype(BF16)
        v_out[:, V7X_LANES * (2 * jj + 1):V7X_LANES * (2 * jj + 2)] = b.astype(BF16)


def _ret_in_kernel(*refs, n_prompt, split_x):
    x, (g_ref, mod_ref, w_ref, cos_ref, sin_ref, q_out, k_out, v_out, g_out) = _read_x(refs, n_prompt, split_x)
    d = x.shape[1]
    e1 = RET_HEADS * RET_DK
    e3 = 2 * e1 + RET_HEADS * RET_DV
    h = _modulate(x, g_ref[...], mod_ref[:, 0:d], mod_ref[:, d:2 * d]).astype(BF16)
    n_chunks = e1 // V7X_LANES
    for sec, out in ((0, q_out), (1, k_out)):
        p = _dot(h, w_ref[:, sec * e1:(sec + 1) * e1])
        for j in range(n_chunks):
            s = j % 2
            xj = p[:, V7X_LANES * j:V7X_LANES * (j + 1)]
            yj = (xj * cos_ref[:, V7X_LANES * s:V7X_LANES * (s + 1)]
                  + pltpu.roll(xj, 64, 1) * sin_ref[:, V7X_LANES * s:V7X_LANES * (s + 1)])
            if sec == 1:
                yj = yj * (RET_DK ** -0.5)
            out[:, V7X_LANES * j:V7X_LANES * (j + 1)] = yj.astype(BF16)
    v_out[...] = _dot(h, w_ref[:, 2 * e1:e3]).astype(BF16)
    g_out[...] = _dot(h, w_ref[:, e3:]).astype(BF16)


def _lru_in_kernel(*refs, n_prompt, split_x):
    x, (g_ref, mod_ref, w_ref, gate_out, xr_out) = _read_x(refs, n_prompt, split_x)
    d = x.shape[1]
    d_rnn = gate_out.shape[1]
    h = _modulate(x, g_ref[...], mod_ref[:, 0:d], mod_ref[:, d:2 * d]).astype(BF16)
    gate_out[...] = _dot(h, w_ref[:, :d_rnn])
    xr_out[...] = _dot(h, w_ref[:, d_rnn:])


def _x_specs(tm, tps, d, split_x):
    if split_x:
        return [pl.BlockSpec((tm, d), lambda i: (jnp.minimum(i, tps - 1), 0)),
                pl.BlockSpec((tm, d), lambda i: (jnp.maximum(i - tps, 0), 0))]
    return [pl.BlockSpec((tm, d), lambda i: (i, 0))]


def _token_specs(tm, tps, d, layer, split_x):
    return _x_specs(tm, tps, d, split_x) + [
        pl.BlockSpec((None, 1, d), lambda i: (layer, 0, 0)),
        pl.BlockSpec((None, None, 1, N_MOD * d), lambda i: (layer, i // tps, 0, 0))]


def _rope_specs(tm, tps, width):
    spec = pl.BlockSpec((None, tm, width), lambda i: (jnp.minimum(i // tps, 1), i % tps, 0))
    return [spec, spec]


def _attend(q_ref, k_ref, v_ref, bias_ref, sink_ref, o_ref, s_ref, p_ref, m_ref):
    nq, nk = q_ref.shape[0], k_ref.shape[0]
    w = V7X_LANES
    low = lax.broadcasted_iota(jnp.int32, (nq, w), 1) < 64

    def score_cols(h, rows, brows):
        cols = []
        for c in range(nk // w):
            piece = s_ref[h, rows, w * c:w * (c + 1)]
            if bias_ref is not None and c in (0, 2):
                piece = piece + bias_ref[brows, w * (c // 2):w * (c // 2 + 1)]
            cols.append(piece)
        return cols

    for h in range(ATT_KV_HEADS):
        hcol = slice(w * h, w * (h + 1))
        qs = []
        for g in range(ATT_GROUPS):
            a = ATT_GROUPS * h + g
            piece = q_ref[:, w * (a // 2):w * (a // 2 + 1)]
            qs.append(jnp.where(low if a % 2 == 0 else jnp.logical_not(low), piece, jnp.zeros_like(piece)))
        s_ref[h] = _dot_nt(jnp.concatenate(qs, axis=0), k_ref[:, hcol])
        snk = [sink_ref[ATT_GROUPS * h + g] * LOG2E for g in range(ATT_GROUPS)]
        for g in range(ATT_GROUPS):
            for r in range(nq // V7X_SUBLANES):
                rows = slice(nq * g + V7X_SUBLANES * r, nq * g + V7X_SUBLANES * (r + 1))
                cols = score_cols(h, rows, slice(V7X_SUBLANES * r, V7X_SUBLANES * (r + 1)))
                mx = functools.reduce(jnp.maximum, cols)
                m = jnp.maximum(jnp.max(mx, axis=-1, keepdims=True), snk[g])
                m_ref[h, rows, :] = jnp.broadcast_to(m, (V7X_SUBLANES, w))
        rb = SOFTMAX_ROWS
        for g in range(ATT_GROUPS):
            for r in range(nq // rb):
                rows = slice(nq * g + rb * r, nq * g + rb * (r + 1))
                mb = m_ref[h, rows, :]
                for c, piece in enumerate(score_cols(h, rows, slice(rb * r, rb * (r + 1)))):
                    p_ref[h, rows, w * c:w * (c + 1)] = jnp.exp2(piece - mb).astype(BF16)
        o = _dot(p_ref[h], v_ref[:, hcol])
        for pair in range(ATT_GROUPS // 2):
            vals = []
            for g in (2 * pair, 2 * pair + 1):
                rows = slice(nq * g, nq * (g + 1))
                og = o[rows]
                rg = pltpu.roll(og, 64, 1)
                eg = jnp.exp2(snk[g] - m_ref[h, rows, :])
                vals.append(og / (rg + eg) if g % 2 == 0 else rg / (og + eg))
            col = (ATT_GROUPS * h) // 2 + pair
            o_ref[:, w * col:w * (col + 1)] = jnp.where(low, vals[0], vals[1]).astype(BF16)


def _attn_scratch(nq, nk):
    rows = ATT_GROUPS * nq
    return [pltpu.VMEM((ATT_KV_HEADS, rows, nk), F32), pltpu.VMEM((ATT_KV_HEADS, rows, nk), BF16),
            pltpu.VMEM((ATT_KV_HEADS, rows, V7X_LANES), F32)]


def _attn_prompt_kernel(sink_ref, q_ref, k_ref, v_ref, o_ref, s_ref, p_ref, m_ref):
    _attend(q_ref, k_ref, v_ref, None, sink_ref, o_ref, s_ref, p_ref, m_ref)


def _attn_sample_kernel(sink_ref, q_ref, kp_ref, kc_ref, kn_ref, vp_ref, vc_ref, vn_ref,
                        kctx_ref, vctx_ref, o_ref, kall, vall, bias_ref, s_ref, p_ref, m_ref):
    n = pl.program_id(1)
    nb = pl.num_programs(1)
    blk = ATT_BLOCK
    row = lax.broadcasted_iota(jnp.int32, (blk, blk), 0)
    col = lax.broadcasted_iota(jnp.int32, (blk, blk), 1)
    bias_ref[:, 0:blk] = jnp.where(jnp.logical_and(col >= row, n > 0), 0.0, NEG_INF).astype(F32)
    bias_ref[:, blk:2 * blk] = jnp.where(jnp.logical_and(col <= row, n < nb - 1), 0.0, NEG_INF).astype(F32)
    for dst, srcs in ((kall, (kp_ref, kc_ref, kn_ref)), (vall, (vp_ref, vc_ref, vn_ref))):
        for j, src in enumerate(srcs):
            dst[blk * j:blk * (j + 1), :] = src[...]

    @pl.when(n == 0)
    def _():
        n_ctx = kctx_ref.shape[0]
        low_ctx = lax.broadcasted_iota(jnp.int32, (n_ctx, V7X_LANES), 1) < 64
        for dst, src, split in ((kall, kctx_ref, _dup_halves), (vall, vctx_ref, _ones_halves)):
            for j in range(src.shape[1] // V7X_LANES):
                a, b = split(src[:, V7X_LANES * j:V7X_LANES * (j + 1)], low_ctx)
                dst[3 * blk:, V7X_LANES * (2 * j):V7X_LANES * (2 * j + 1)] = a.astype(BF16)
                dst[3 * blk:, V7X_LANES * (2 * j + 1):V7X_LANES * (2 * j + 2)] = b.astype(BF16)

    _attend(q_ref, kall, vall, bias_ref, sink_ref, o_ref, s_ref, p_ref, m_ref)


def _ret_kernel(lg_ref, q_ref, k_ref, v_ref, g_ref, *rest, t_len, c_len, prompt):
    if prompt:
        gn_ref, o_ref, sfin_ref, obuf, sf, sb = rest
    else:
        s0_ref, gn_ref, o_ref, obuf, sf, sb = rest
    nc = t_len // c_len
    h = pl.program_id(1)
    lgf = lg_ref[0, h]
    lgb = lg_ref[1, h]
    row = lax.broadcasted_iota(jnp.int32, (c_len, c_len), 0).astype(F32)
    col = lax.broadcasted_iota(jnp.int32, (c_len, c_len), 1).astype(F32)
    diff = row - col
    decay_f = jnp.where(diff >= 0, jnp.exp(lgf * jnp.maximum(diff, 0.0)), 0.0)
    decay_b = jnp.where(diff < 0, jnp.exp(lgb * jnp.maximum(-diff, 0.0)), 0.0)
    idx = lax.broadcasted_iota(jnp.int32, (c_len, 1), 0).astype(F32)
    one = jnp.ones((1, 1), F32)
    wq_f, ws_f, gc_f = jnp.exp(lgf * (idx + 1.0)), jnp.exp(lgf * (c_len - 1.0 - idx)), jnp.exp(lgf * c_len * one)
    wq_b, ws_b, gc_b = jnp.exp(lgb * (c_len - idx)), jnp.exp(lgb * idx), jnp.exp(lgb * c_len * one)
    if prompt:
        sf[...] = jnp.zeros(sf.shape, F32)
        sb[...] = jnp.zeros(sb.shape, F32)
    else:
        sf[...] = s0_ref[0]
        sb[...] = s0_ref[1]

    def rows(c):
        start = c * c_len
        return pl.ds(start if isinstance(start, int) else pl.multiple_of(start, c_len), c_len)

    def chunk(sl, state, decay, wq, ws, gc):
        qc, kc, vc = q_ref[sl, :], k_ref[sl, :], v_ref[sl, :]
        s_prev = state[...]
        sc = (_dot_nt(qc, kc) * decay).astype(BF16)
        o = _dot(sc, vc) + _dot(qc, s_prev.astype(BF16)) * wq
        kw = (kc.astype(F32) * ws).astype(BF16)
        state[...] = gc * s_prev + _dot_tn(kw, vc)
        return o

    def fwd(sl):
        return chunk(sl, sf, decay_f, wq_f, ws_f, gc_f)

    def bwd(sl):
        return chunk(sl, sb, decay_b, wq_b, ws_b, gc_b)

    def finish(sl, o):
        mu = jnp.mean(o, axis=-1, keepdims=True)
        dev = o - mu
        var = jnp.mean(dev * dev, axis=-1, keepdims=True)
        on = dev * lax.rsqrt(var + RMS_EPS) * gn_ref[...]
        o_ref[sl, :] = (_silu(g_ref[sl, :].astype(F32)) * on).astype(BF16)

    half = nc // 2

    def first_half(i, carry):
        sl_f, sl_b = rows(i), rows(nc - 1 - i)
        obuf[sl_f, :] = fwd(sl_f)
        obuf[sl_b, :] = bwd(sl_b)
        return carry

    lax.fori_loop(0, half, first_half, 0)
    if nc % 2:
        sl = rows(half)
        finish(sl, fwd(sl) + bwd(sl))

    def second_half(i, carry):
        sl_f, sl_b = rows(i), rows(nc - 1 - i)
        finish(sl_f, fwd(sl_f) + obuf[sl_f, :])
        finish(sl_b, bwd(sl_b) + obuf[sl_b, :])
        return carry

    lax.fori_loop(half + nc % 2, nc, second_half, 0)
    if prompt:
        sfin_ref[0] = sf[...]
        sfin_ref[1] = sb[...]


def _softplus(x):
    return jnp.maximum(x, 0.0) + jnp.log1p(jnp.exp(-jnp.abs(x)))


def _lru_kernel(xr_ref, prev_ref, next_ref, gate_ref, cw_ref, cb_ref, wg_ref, br_ref, bi_ref, lam_ref,
                h0_ref, y_ref, hfin_ref, hf_buf, a_s, u_s, hc, *, nc):
    tc, d_rnn = xr_ref.shape
    bw = d_rnn // LRU_BLOCKS
    s = pl.program_id(1)
    is_b = s >= nc
    c = jnp.where(is_b, 2 * nc - 1 - s, s)

    x = xr_ref[...]
    rowi = lax.broadcasted_iota(jnp.int32, (tc, d_rnn), 0)
    p1 = jnp.where(c > 0, prev_ref[V7X_SUBLANES - 1:V7X_SUBLANES, :], 0.0)
    p2 = jnp.where(c > 0, prev_ref[V7X_SUBLANES - 2:V7X_SUBLANES - 1, :], 0.0)
    n1 = jnp.where(c < nc - 1, next_ref[0:1, :], 0.0)
    xm1 = jnp.where(rowi == 0, p1, pltpu.roll(x, 1, 0))
    xm2 = jnp.where(rowi == 0, p2, jnp.where(rowi == 1, p1, pltpu.roll(x, 2, 0)))
    xp1 = jnp.where(rowi == tc - 1, n1, pltpu.roll(x, tc - 1, 0))
    xc = (cw_ref[0:1, :] * xm2 + cw_ref[1:2, :] * xm1 + cw_ref[2:3, :] * x + cw_ref[3:4, :] * xp1
          + cb_ref[...])

    xcb = xc.astype(BF16)
    sp = _softplus(-lam_ref[...])
    for n in range(LRU_BLOCKS):
        cs = slice(bw * n, bw * (n + 1))
        z = _dot(xcb[:, cs], wg_ref[n])
        r = jax.nn.sigmoid(z[:, :bw] + br_ref[:, cs])
        i = jax.nn.sigmoid(z[:, bw:] + bi_ref[:, cs])
        log_a = -LRU_C * r * sp[:, cs]
        a = jnp.exp(log_a)
        u = jnp.sqrt(-jnp.tanh(log_a) * (a * a + 1.0)) * (i * xc[:, cs])
        a_s[:, cs] = a
        u_s[:, cs] = u

    @pl.when(s == 0)
    def _():
        hc[...] = h0_ref[0:1, :]

    @pl.when(s == nc)
    def _():
        hc[...] = h0_ref[1:2, :]

    base = pl.multiple_of(c * tc, tc)

    @pl.when(jnp.logical_not(is_b))
    def _():
        def step(t, h):
            h = a_s[pl.ds(t, 1), :] * h + u_s[pl.ds(t, 1), :]
            hf_buf[pl.ds(base + t, 1), :] = h
            return h
        h = lax.fori_loop(0, tc, step, hc[...], unroll=8)
        hc[...] = h

        @pl.when(s == nc - 1)
        def _():
            hfin_ref[0:1, :] = h

    @pl.when(is_b)
    def _():
        def step(k, h):
            t = tc - 1 - k
            h = a_s[pl.ds(t, 1), :] * h + u_s[pl.ds(t, 1), :]
            u_s[pl.ds(t, 1), :] = h
            return h
        h = lax.fori_loop(0, tc, step, hc[...], unroll=8)
        hc[...] = h
        rec = hf_buf[pl.ds(base, tc), :] + u_s[...]
        y_ref[...] = (jax.nn.gelu(gate_ref[...], approximate=True) * rec).astype(BF16)

        @pl.when(s == 2 * nc - 1)
        def _():
            hfin_ref[1:2, :] = h


def _out_mlp_kernel(*refs, n_prompt, split_x, split_out):
    x, rest = _read_x(refs, n_prompt, split_x)
    yp_ref, ys_ref, mod_ref, gm_ref, wo_ref, wu_ref, wd_ref = rest[:7]
    d = x.shape[1]
    d_ff = wu_ref.shape[1]
    is_prompt = pl.program_id(0) < n_prompt
    y = jnp.where(is_prompt, yp_ref[...], ys_ref[...])
    x1 = x + mod_ref[:, 2 * d:3 * d] * _dot(y, wo_ref[...])
    hn = _modulate(x1, gm_ref[...], mod_ref[:, 3 * d:4 * d], mod_ref[:, 4 * d:5 * d]).astype(BF16)
    acc = jnp.zeros(x1.shape, F32)
    for c in range(d_ff // FF_CHUNK):
        hh = jnp.maximum(_dot(hn, wu_ref[:, FF_CHUNK * c:FF_CHUNK * (c + 1)]), 0.0)
        acc = acc + _dot((hh * hh).astype(BF16), wd_ref[FF_CHUNK * c:FF_CHUNK * (c + 1), :])
    out = x1 + mod_ref[:, 5 * d:6 * d] * acc
    if split_out:
        op_ref, os_ref = rest[7:]

        @pl.when(is_prompt)
        def _():
            op_ref[...] = out

        @pl.when(jnp.logical_not(is_prompt))
        def _():
            os_ref[...] = out
    else:
        rest[7][...] = out


class _Dims:
    def __init__(self, bp, lp, bs, ts, d):
        assert bp * lp == ts, "prompt tokens must fill exactly one segment"
        self.bp, self.lp, self.bs, self.ts, self.d = bp, lp, bs, ts, d
        self.nseg = 1 + bs
        self.n = self.nseg * ts
        self.tm = min(TOKEN_TILE, ts)
        assert ts % self.tm == 0 and ts % ATT_BLOCK == 0
        assert ts % RET_KERNEL_CHUNK == 0 and lp % min(RET_KERNEL_CHUNK, lp) == 0
        self.tps = ts // self.tm
        self.ntiles = self.n // self.tm


def _in_kernel(body, dm, split_x):
    return functools.partial(body, n_prompt=dm.tps, split_x=split_x)


def _attn_layer(dm, layer, slot, xs, norm_mix, mods, w_in, q_gain, k_gain, sink, cache_k, cache_v, rope):
    d, tm, tps, n = dm.d, dm.tm, dm.tps, dm.n
    split_x = len(xs) == 2
    hd = ATT_HEAD_DIM
    q_w, kv_w = ATT_HEADS * hd, ATT_KV_HEADS * hd
    gain = jnp.concatenate([jnp.tile(q_gain, ATT_HEADS), jnp.tile(k_gain, ATT_KV_HEADS)])[None, :]
    gw = q_w + kv_w
    assert (q_w + kv_w) % gw == 0 and gw % V7X_LANES == 0 and gw // hd <= V7X_LANES
    head_of_lane = np.arange(gw) // hd
    hsum = jnp.asarray(head_of_lane[:, None] == np.arange(V7X_LANES)[None, :], BF16)
    hexp = jnp.asarray(np.arange(V7X_LANES)[:, None] == head_of_lane[None, :], BF16)
    cos, sin = rope
    tok = lambda w: pl.BlockSpec((tm, w), lambda i: (i, 0))
    ptok = lambda w: pl.BlockSpec((tm, w), lambda i: (jnp.minimum(i, tps - 1), 0))
    q, k, v, kf, vf = pl.pallas_call(
        _in_kernel(_attn_in_kernel, dm, split_x),
        grid=(dm.ntiles,),
        in_specs=_token_specs(tm, tps, d, layer, split_x) + [
            _resident(w_in.shape), _resident(gain.shape)] + _rope_specs(tm, tps, V7X_LANES) + [
            _resident(hsum.shape), _resident(hexp.shape)],
        out_specs=[tok(q_w), tok(2 * kv_w), tok(2 * kv_w), ptok(kv_w), ptok(kv_w)],
        out_shape=[jax.ShapeDtypeStruct((n, q_w), BF16), jax.ShapeDtypeStruct((n, 2 * kv_w), BF16),
                   jax.ShapeDtypeStruct((n, 2 * kv_w), BF16), jax.ShapeDtypeStruct((dm.ts, kv_w), F32),
                   jax.ShapeDtypeStruct((dm.ts, kv_w), F32)],
        compiler_params=_cparams(("arbitrary",)),
        name="attn_in",
    )(*xs, norm_mix, mods, w_in, gain, cos, sin, hsum, hexp)

    smem = pl.BlockSpec(memory_space=pltpu.SMEM)
    lp, bp, bs, ts = dm.lp, dm.bp, dm.bs, dm.ts
    y_p = pl.pallas_call(
        _attn_prompt_kernel,
        grid=(bp,),
        in_specs=[smem, pl.BlockSpec((lp, q_w), lambda b: (b, 0)),
                  pl.BlockSpec((lp, 2 * kv_w), lambda b: (b, 0)),
                  pl.BlockSpec((lp, 2 * kv_w), lambda b: (b, 0))],
        out_specs=pl.BlockSpec((lp, q_w), lambda b: (b, 0)),
        out_shape=jax.ShapeDtypeStruct((ts, q_w), BF16),
        scratch_shapes=_attn_scratch(lp, lp),
        compiler_params=_cparams(("arbitrary",)),
        name="attn_prompt",
    )(sink, q, k, v)

    blk = ATT_BLOCK
    nb = ts // blk
    cur = lambda b, j: ((1 + b) * nb + j, 0)
    prv = lambda b, j: ((1 + b) * nb + jnp.maximum(j - 1, 0), 0)
    nxt = lambda b, j: ((1 + b) * nb + jnp.minimum(j + 1, nb - 1), 0)
    past = cache_k.shape[2]
    nk = 3 * blk + past
    ctx_spec = pl.BlockSpec((None, None, past, kv_w), lambda b, j: (b, slot, 0, 0))
    kv_spec = lambda f: pl.BlockSpec((blk, 2 * kv_w), f)
    y_s = pl.pallas_call(
        _attn_sample_kernel,
        grid=(bs, nb),
        in_specs=[smem, pl.BlockSpec((blk, q_w), cur), kv_spec(prv), kv_spec(cur), kv_spec(nxt),
                  kv_spec(prv), kv_spec(cur), kv_spec(nxt), ctx_spec, ctx_spec],
        out_specs=pl.BlockSpec((blk, q_w), lambda b, j: (b * nb + j, 0)),
        out_shape=jax.ShapeDtypeStruct((bs * ts, q_w), BF16),
        scratch_shapes=[pltpu.VMEM((nk, 2 * kv_w), BF16), pltpu.VMEM((nk, 2 * kv_w), BF16),
                        pltpu.VMEM((blk, 2 * blk), F32)] + _attn_scratch(blk, nk),
        compiler_params=_cparams(("arbitrary", "arbitrary")),
        name="attn_sample",
    )(sink, q, k, k, k, v, v, v, cache_k, cache_v)
    return y_p, y_s, kf, vf


def _ret_layer(dm, layer, slot, xs, norm_mix, mods, w_in, gn_gain, log_decay, state, rope):
    d, tm, tps, n = dm.d, dm.tm, dm.tps, dm.n
    split_x = len(xs) == 2
    e1, ev = RET_HEADS * RET_DK, RET_HEADS * RET_DV
    cos, sin = rope
    tok = lambda w: pl.BlockSpec((tm, w), lambda i: (i, 0))
    q, k, v, g = pl.pallas_call(
        _in_kernel(_ret_in_kernel, dm, split_x),
        grid=(dm.ntiles,),
        in_specs=(_token_specs(tm, tps, d, layer, split_x) + [_resident(w_in.shape)]
                  + _rope_specs(tm, tps, RET_DK)),
        out_specs=[tok(e1), tok(e1), tok(ev), tok(ev)],
        out_shape=[jax.ShapeDtypeStruct((n, e1), BF16), jax.ShapeDtypeStruct((n, e1), BF16),
                   jax.ShapeDtypeStruct((n, ev), BF16), jax.ShapeDtypeStruct((n, ev), BF16)],
        compiler_params=_cparams(("arbitrary",)),
        name="ret_in",
    )(*xs, norm_mix, mods, w_in, cos, sin)

    smem = pl.BlockSpec(memory_space=pltpu.SMEM)
    gn = gn_gain[None, :]
    lp, bp, bs, ts = dm.lp, dm.bp, dm.bs, dm.ts
    state_spec = lambda f: pl.BlockSpec((None, None, 2, None, RET_DK, RET_DV), f)

    def call(t_len, nbatch, seg0, prompt, name):
        rows = lambda w: pl.BlockSpec((t_len, w), lambda b, h: (seg0 + b, h))
        in_specs = [smem, rows(RET_DK), rows(RET_DK), rows(RET_DV), rows(RET_DV)]
        args = [log_decay, q, k, v, g]
        out_shape = [jax.ShapeDtypeStruct((nbatch * t_len, ev), BF16)]
        out_specs = [pl.BlockSpec((t_len, RET_DV), lambda b, h: (b, h))]
        if prompt:
            out_shape.append(jax.ShapeDtypeStruct((nbatch, 1, 2, RET_HEADS, RET_DK, RET_DV), F32))
            out_specs.append(state_spec(lambda b, h: (b, 0, 0, h, 0, 0)))
        else:
            in_specs.append(state_spec(lambda b, h: (b, slot, 0, h, 0, 0)))
            args.append(state)
        in_specs.append(pl.BlockSpec((1, RET_DV), lambda b, h: (0, h)))
        args.append(gn)
        return pl.pallas_call(
            functools.partial(_ret_kernel, t_len=t_len, c_len=min(RET_KERNEL_CHUNK, t_len), prompt=prompt),
            grid=(nbatch, RET_HEADS),
            in_specs=in_specs, out_specs=out_specs, out_shape=out_shape,
            scratch_shapes=[pltpu.VMEM((t_len, RET_DV), F32), pltpu.VMEM((RET_DK, RET_DV), F32),
                            pltpu.VMEM((RET_DK, RET_DV), F32)],
            compiler_params=_cparams(("arbitrary", "arbitrary")),
            name=name,
        )(*args)

    y_p, s_new = call(lp, bp, 0, True, "ret_prompt")
    (y_s,) = call(ts, bs, 1, False, "ret_sample")
    return y_p, y_s, s_new


def _lru_layer(dm, layer, slot, xs, norm_mix, mods, w_in, conv_w, conv_b, w_r, b_r, w_i, b_i, lam, state):
    d, tm, tps, n = dm.d, dm.tm, dm.tps, dm.n
    split_x = len(xs) == 2
    d_rnn = w_in.shape[1] // 2
    tok = lambda w: pl.BlockSpec((tm, w), lambda i: (i, 0))
    gate, xr = pl.pallas_call(
        _in_kernel(_lru_in_kernel, dm, split_x),
        grid=(dm.ntiles,),
        in_specs=_token_specs(tm, tps, d, layer, split_x) + [_resident(w_in.shape)],
        out_specs=[tok(d_rnn), tok(d_rnn)],
        out_shape=[jax.ShapeDtypeStruct((n, d_rnn), F32), jax.ShapeDtypeStruct((n, d_rnn), F32)],
        compiler_params=_cparams(("arbitrary",)),
        name="lru_in",
    )(*xs, norm_mix, mods, w_in)

    wg = jnp.concatenate([w_r, w_i], axis=-1).astype(BF16)
    vec = lambda a: a[:, None, :]
    lp, bp, bs, ts = dm.lp, dm.bp, dm.bs, dm.ts
    sub = V7X_SUBLANES

    def call(t_len, nbatch, row0, h0, h0_slot, name):
        tc = t_len if t_len <= 256 else min(512, t_len // 2)
        nc = t_len // tc
        cb = row0 // tc
        chunk = lambda s: jnp.where(s >= nc, 2 * nc - 1 - s, s)
        late = lambda s: jnp.where(s >= nc, 2 * nc - 1 - s, nc - 1)
        dirv = lambda s: jnp.where(s >= nc, 1, 0)
        r8 = tc // sub
        first8 = lambda b: (row0 + b * t_len) // sub
        prev8 = lambda b, s: (jnp.maximum(first8(b) + chunk(s) * r8 - 1, first8(b)), 0)
        next8 = lambda b, s: (jnp.minimum(first8(b) + (chunk(s) + 1) * r8, first8(b) + t_len // sub - 1), 0)
        dspec = lambda shape: pl.BlockSpec((None,) + shape, lambda b, s: (dirv(s),) + (0,) * len(shape))
        return pl.pallas_call(
            functools.partial(_lru_kernel, nc=nc),
            grid=(nbatch, 2 * nc),
            in_specs=[pl.BlockSpec((tc, d_rnn), lambda b, s: (cb + b * nc + chunk(s), 0)),
                      pl.BlockSpec((sub, d_rnn), prev8), pl.BlockSpec((sub, d_rnn), next8),
                      pl.BlockSpec((tc, d_rnn), lambda b, s: (cb + b * nc + late(s), 0)),
                      pl.BlockSpec((CONV_W, d_rnn), lambda b, s: (0, 0)),
                      pl.BlockSpec((1, d_rnn), lambda b, s: (0, 0)),
                      dspec(wg.shape[1:]), dspec((1, d_rnn)), dspec((1, d_rnn)), dspec((1, d_rnn)),
                      pl.BlockSpec((None, None, 2, d_rnn), lambda b, s: (b, h0_slot, 0, 0))],
            out_specs=[pl.BlockSpec((tc, d_rnn), lambda b, s: (b * nc + late(s), 0)),
                       pl.BlockSpec((None, 2, d_rnn), lambda b, s: (b, 0, 0))],
            out_shape=[jax.ShapeDtypeStruct((nbatch * t_len, d_rnn), BF16),
                       jax.ShapeDtypeStruct((nbatch, 2, d_rnn), F32)],
            scratch_shapes=[pltpu.VMEM((t_len, d_rnn), F32), pltpu.VMEM((tc, d_rnn), F32),
                            pltpu.VMEM((tc, d_rnn), F32), pltpu.VMEM((1, d_rnn), F32)],
            compiler_params=_cparams(("arbitrary", "arbitrary")),
            name=name,
        )(xr, xr, xr, gate, conv_w, conv_b[None, :], wg, vec(b_r), vec(b_i), vec(lam), h0)

    y_p, h_new = call(lp, bp, 0, jnp.zeros((bp, 1, 2, d_rnn), F32), 0, "lru_prompt")
    y_s, _ = call(ts, bs, ts, state, slot, "lru_sample")
    return y_p, y_s, h_new


def _out_mlp(dm, layer, xs, y_p, y_s, norm_mlp, mods, w_out, w_up, w_down, split_out):
    d, tm, tps, n = dm.d, dm.tm, dm.tps, dm.n
    split_x = len(xs) == 2
    din = w_out.shape[0]
    first = lambda i: (jnp.minimum(i, tps - 1), 0)
    second = lambda i: (jnp.maximum(i - tps, 0), 0)
    if split_out:
        out_specs = [pl.BlockSpec((tm, d), first), pl.BlockSpec((tm, d), second)]
        out_shape = [jax.ShapeDtypeStruct((dm.ts, d), F32), jax.ShapeDtypeStruct((n - dm.ts, d), F32)]
    else:
        out_specs = pl.BlockSpec((tm, d), lambda i: (i, 0))
        out_shape = jax.ShapeDtypeStruct((n, d), F32)
    return pl.pallas_call(
        functools.partial(_out_mlp_kernel, n_prompt=tps, split_x=split_x, split_out=split_out),
        grid=(dm.ntiles,),
        in_specs=_x_specs(tm, tps, d, split_x) + [
            pl.BlockSpec((tm, din), first), pl.BlockSpec((tm, din), second),
            pl.BlockSpec((None, None, 1, N_MOD * d), lambda i: (layer, i // tps, 0, 0)),
            pl.BlockSpec((None, 1, d), lambda i: (layer, 0, 0)),
            _resident(w_out.shape), _resident(w_up.shape), _resident(w_down.shape)],
        out_specs=out_specs, out_shape=out_shape,
        compiler_params=_cparams(("arbitrary",)),
        name="out_mlp",
    )(*xs, y_p, y_s, mods, norm_mlp, w_out, w_up, w_down)


def kernel(x_prompt, x_sample, cache_attn_k, cache_attn_v, state_ret, state_lru, c, c_ctx, norm_mix, norm_mlp, w_ada, b_ada, w_up, w_down, attn_w_in, attn_w_out, attn_q_gain, attn_k_gain, attn_sink, ret_w_in, ret_w_out, ret_gn_gain, ret_log_decay, lru_w_in, lru_conv_w, lru_conv_b, lru_w_r, lru_b_r, lru_w_i, lru_b_i, lru_lambda, lru_w_out):
    bp, lp, d = x_prompt.shape
    bs, ts, _ = x_sample.shape
    dm = _Dims(bp, lp, bs, ts, d)
    depth = w_ada.shape[0]
    assert dm.nseg <= MOD_ROWS

    xs = (x_prompt.reshape(ts, d), x_sample.reshape(bs * ts, d))
    cvec = jnp.concatenate([c_ctx[None, :], c, jnp.zeros((MOD_ROWS - dm.nseg, d), F32)], axis=0)
    mods = _modulation(cvec, w_ada, b_ada)[:, :dm.nseg].reshape(depth, dm.nseg, 1, N_MOD * d)
    nmix = norm_mix[:, None, :]
    nmlp = norm_mlp[:, None, :]

    rope_attn = tuple(jnp.tile(t, (1, 1, 2)) for t in _rope_tables(ts, ATT_HEAD_DIM // 2))
    rope_ret = _rope_tables(ts, RET_DK // 2)
    kv_w = ATT_KV_HEADS * ATT_HEAD_DIM
    past = cache_attn_k.shape[2]
    ck = cache_attn_k.reshape(bs, -1, past, kv_w)
    cv = cache_attn_v.reshape(bs, -1, past, kv_w)

    new_k, new_v, new_ret, new_lru = [], [], [], []
    for layer in range(depth):
        kind, slot = layer % 3, layer // 3
        if kind == 0:
            y_p, y_s, kf, vf = _attn_layer(dm, layer, slot, xs, nmix, mods, attn_w_in[slot].astype(BF16),
                                           attn_q_gain[slot], attn_k_gain[slot], attn_sink[slot],
                                           ck, cv, rope_attn)
            new_k.append(kf.reshape(bp, lp, ATT_KV_HEADS, ATT_HEAD_DIM))
            new_v.append(vf.reshape(bp, lp, ATT_KV_HEADS, ATT_HEAD_DIM))
            w_out = attn_w_out[slot]
        elif kind == 1:
            y_p, y_s, s_new = _ret_layer(dm, layer, slot, xs, nmix, mods, ret_w_in[slot].astype(BF16),
                                         ret_gn_gain[slot], ret_log_decay[slot], state_ret, rope_ret)
            new_ret.append(s_new)
            w_out = ret_w_out[slot]
        else:
            y_p, y_s, h_new = _lru_layer(dm, layer, slot, xs, nmix, mods, lru_w_in[slot].astype(BF16),
                                         lru_conv_w[slot], lru_conv_b[slot], lru_w_r[slot], lru_b_r[slot],
                                         lru_w_i[slot], lru_b_i[slot], lru_lambda[slot], state_lru)
            new_lru.append(h_new)
            w_out = lru_w_out[slot]
        last = layer == depth - 1
        out = _out_mlp(dm, layer, xs, y_p, y_s, nmlp, mods, w_out.astype(BF16),
                       w_up[layer].astype(BF16), w_down[layer].astype(BF16), split_out=last)
        xs = tuple(out) if last else (out,)

    y_prompt = xs[0].reshape(bp, lp, d)
    y_sample = xs[1].reshape(bs, ts, d)
    return (y_prompt, y_sample, jnp.stack(new_k, axis=1), jnp.stack(new_v, axis=1),
            jnp.concatenate(new_ret, axis=1), jnp.stack(new_lru, axis=1))
```

```python
import functools

import jax
import jax.numpy as jnp
import numpy as np
from jax import lax
from jax.experimental import pallas as pl
from jax.experimental.pallas import tpu as pltpu

F32 = jnp.float32
BF16 = jnp.bfloat16

N_MOD = 6
RMS_EPS = 1e-6
ROPE_BASE = 10000.0
NEG_INF = -1e30
LOG2E = 1.4426950408889634
GRID_W = 64
ATT_HEADS = 16
ATT_KV_HEADS = 4
ATT_HEAD_DIM = 64
ATT_GROUPS = ATT_HEADS // ATT_KV_HEADS
ATT_BLOCK = 128
RET_HEADS = 4
RET_DK = 256
RET_DV = 512
LRU_BLOCKS = 8
CONV_W = 4
LRU_C = 8.0

V7X_LANES = 128
V7X_SUBLANES = 8
V7X_VMEM_BYTES = 64 * 1024 * 1024
VMEM_LIMIT = V7X_VMEM_BYTES - 8 * 1024 * 1024

TOKEN_TILE = 512
MOD_ROWS = 16
FF_CHUNK = 1024
RET_KERNEL_CHUNK = 256


def _cparams(sem):
    return pltpu.CompilerParams(dimension_semantics=sem, vmem_limit_bytes=VMEM_LIMIT)


def _resident(shape):
    zeros = (0,) * len(shape)
    return pl.BlockSpec(shape, lambda *_: zeros, pipeline_mode=pl.Buffered(1))


def _dot(a, b):
    return jnp.dot(a, b, preferred_element_type=F32)


def _dot_nt(a, b):
    return lax.dot_general(a, b, (((1,), (1,)), ((), ())), preferred_element_type=F32)


def _dot_tn(a, b):
    return lax.dot_general(a, b, (((0,), (0,)), ((), ())), preferred_element_type=F32)


def _silu(x):
    return x * jax.nn.sigmoid(x)


def _modulate(x, g, shift, scale):
    ms = jnp.mean(x * x, axis=-1, keepdims=True)
    y = x * lax.rsqrt(ms + RMS_EPS) * g
    return y * (1.0 + scale) + shift


def _split_bf16(x):
    hi = x.astype(BF16)
    lo = (x - hi.astype(F32)).astype(BF16)
    return hi, lo


def _read_x(refs, n_prompt, split_x):
    if split_x:
        x = jnp.where(pl.program_id(0) < n_prompt, refs[0][...], refs[1][...])
        return x, refs[2:]
    return refs[0][...], refs[1:]


def _mod_kernel(c_ref, w_ref, b_ref, o_ref):
    s = _silu(c_ref[...]).astype(BF16)
    o_ref[...] = _dot(s, w_ref[...].astype(BF16)) + b_ref[...]


def _modulation(cvec, w_ada, b_ada):
    depth, d, w = w_ada.shape
    tn = 1536
    return pl.pallas_call(
        _mod_kernel,
        grid=(depth, w // tn),
        in_specs=[pl.BlockSpec((MOD_ROWS, d), lambda l, j: (0, 0)),
                  pl.BlockSpec((None, d, tn), lambda l, j: (l, 0, j)),
                  pl.BlockSpec((None, 1, tn), lambda l, j: (l, 0, j))],
        out_specs=pl.BlockSpec((None, MOD_ROWS, tn), lambda l, j: (l, 0, j)),
        out_shape=jax.ShapeDtypeStruct((depth, MOD_ROWS, w), F32),
        compiler_params=_cparams(("arbitrary", "arbitrary")),
        name="modulation",
    )(cvec, w_ada, b_ada.reshape(depth, 1, w))


def _rope_tables(t_len, sect):
    half = sect // 2
    inv = ROPE_BASE ** (-jnp.arange(half, dtype=F32) / half)
    t = jnp.arange(t_len, dtype=jnp.int32)

    def one(pos):
        ang = pos.astype(F32)[:, None] * inv[None, :]
        c, s = jnp.cos(ang), jnp.sin(ang)
        return jnp.concatenate([c, c], axis=1), jnp.concatenate([-s, s], axis=1)

    cr, sr = one(t // GRID_W)
    cc, sc = one(t % GRID_W)
    cos = jnp.concatenate([cr, cc], axis=1)
    sin = jnp.concatenate([sr, sc], axis=1)
    return (jnp.stack([jnp.ones_like(cos), cos]), jnp.stack([jnp.zeros_like(sin), sin]))


def _dup_halves(x, lane_low):
    r = pltpu.roll(x, 64, 1)
    return jnp.where(lane_low, x, r), jnp.where(lane_low, r, x)


def _attn_in_kernel(*refs, n_prompt, split_x):
    x, rest = _read_x(refs, n_prompt, split_x)
    (g_ref, mod_ref, w_ref, gain_ref, cos_ref, sin_ref, hsum_ref, hexp_ref,
     q_out, k_out, v_out, kf_out, vf_out, proj_s) = rest
    tm, d = x.shape
    i = pl.program_id(0)
    qk_w = hsum_ref.shape[0]
    n_q = ATT_HEADS * ATT_HEAD_DIM // V7X_LANES

    @pl.when(i == 0)
    def _():
        proj_s[1] = jnp.zeros(proj_s.shape[1:], F32)

    def epilogue(proj):
        lane = lax.broadcasted_iota(jnp.int32, (tm, V7X_LANES), 1)
        first = (lane % 32) < 16
        low = lane < 64
        cos = cos_ref[...]
        sin = sin_ref[...]
        qk = proj[:, :qk_w]
        ssum = _dot((qk * qk).astype(BF16), hsum_ref[...])
        r_hi, r_lo = _split_bf16(lax.rsqrt(ssum * (1.0 / ATT_HEAD_DIM) + RMS_EPS))
        qkn = qk * (_dot(r_hi, hexp_ref[...]) + _dot(r_lo, hexp_ref[...])) * gain_ref[...]
        for j in range(qk_w // V7X_LANES):
            xj = qkn[:, V7X_LANES * j:V7X_LANES * (j + 1)]
            rot = jnp.where(first, pltpu.roll(xj, V7X_LANES - 16, 1), pltpu.roll(xj, 16, 1))
            yj = xj * cos + rot * sin
            if j < n_q:
                q_out[:, V7X_LANES * j:V7X_LANES * (j + 1)] = (yj * (ATT_HEAD_DIM ** -0.5 * LOG2E)).astype(BF16)
            else:
                jj = j - n_q
                kf_out[:, V7X_LANES * jj:V7X_LANES * (jj + 1)] = yj
                a, b = _dup_halves(yj, low)
                k_out[:, V7X_LANES * (2 * jj):V7X_LANES * (2 * jj + 1)] = a.astype(BF16)
                k_out[:, V7X_LANES * (2 * jj + 1):V7X_LANES * (2 * jj + 2)] = b.astype(BF16)
        v = proj[:, qk_w:]
        vf_out[...] = v
        for jj in range(v.shape[1] // V7X_LANES):
            a, b = _dup_halves(v[:, V7X_LANES * jj:V7X_LANES * (jj + 1)], low)
            v_out[:, V7X_LANES * (2 * jj):V7X_LANES * (2 * jj + 1)] = a.astype(BF16)
            v_out[:, V7X_LANES * (2 * jj + 1):V7X_LANES * (2 * jj + 2)] = b.astype(BF16)

    def step(slot):
        epilogue(proj_s[1 - slot])
        h = _modulate(x, g_ref[...], mod_ref[:, 0:d], mod_ref[:, d:2 * d]).astype(BF16)
        proj_s[slot] = _dot(h, w_ref[...])

    @pl.when(i % 2 == 0)
    def _():
        step(0)

    @pl.when(i % 2 == 1)
    def _():
        step(1)


def _ret_in_kernel(*refs, n_prompt, split_x):
    x, (g_ref, mod_ref, w_ref, cos_ref, sin_ref, q_out, k_out, v_out, g_out) = _read_x(refs, n_prompt, split_x)
    d = x.shape[1]
    e1 = RET_HEADS * RET_DK
    e3 = 2 * e1 + RET_HEADS * RET_DV
    h = _modulate(x, g_ref[...], mod_ref[:, 0:d], mod_ref[:, d:2 * d]).astype(BF16)
    n_chunks = e1 // V7X_LANES
    for sec, out in ((0, q_out), (1, k_out)):
        p = _dot(h, w_ref[:, sec * e1:(sec + 1) * e1])
        for j in range(n_chunks):
            s = j % 2
            xj = p[:, V7X_LANES * j:V7X_LANES * (j + 1)]
            yj = (xj * cos_ref[:, V7X_LANES * s:V7X_LANES * (s + 1)]
                  + pltpu.roll(xj, 64, 1) * sin_ref[:, V7X_LANES * s:V7X_LANES * (s + 1)])
            if sec == 1:
                yj = yj * (RET_DK ** -0.5)
            out[:, V7X_LANES * j:V7X_LANES * (j + 1)] = yj.astype(BF16)
    v_out[...] = _dot(h, w_ref[:, 2 * e1:e3]).astype(BF16)
    g_out[...] = _dot(h, w_ref[:, e3:]).astype(BF16)


def _lru_in_kernel(*refs, n_prompt, split_x):
    x, (g_ref, mod_ref, w_ref, act_out, xr_out) = _read_x(refs, n_prompt, split_x)
    d = x.shape[1]
    d_rnn = act_out.shape[1]
    h = _modulate(x, g_ref[...], mod_ref[:, 0:d], mod_ref[:, d:2 * d]).astype(BF16)
    act_out[...] = jax.nn.gelu(_dot(h, w_ref[:, :d_rnn]), approximate=True)
    xr_out[...] = _dot(h, w_ref[:, d_rnn:])


def _x_specs(tm, tps, d, split_x, tile=lambda i: i):
    if split_x:
        return [pl.BlockSpec((tm, d), lambda i: (jnp.minimum(tile(i), tps - 1), 0)),
                pl.BlockSpec((tm, d), lambda i: (jnp.maximum(tile(i) - tps, 0), 0))]
    return [pl.BlockSpec((tm, d), lambda i: (tile(i), 0))]


def _token_specs(tm, tps, d, layer, split_x, tile=lambda i: i):
    return _x_specs(tm, tps, d, split_x, tile) + [
        pl.BlockSpec((None, 1, d), lambda i: (layer, 0, 0)),
        pl.BlockSpec((None, None, 1, N_MOD * d), lambda i: (layer, tile(i) // tps, 0, 0))]


def _rope_specs(tm, tps, width, tile=lambda i: i):
    spec = pl.BlockSpec((None, tm, width), lambda i: (jnp.minimum(tile(i) // tps, 1), tile(i) % tps, 0))
    return [spec, spec]


def _attend(q_ref, k_ref, v_ref, bias, sink_ref, o_ref):
    nq = q_ref.shape[0]
    w = V7X_LANES
    low = lax.broadcasted_iota(jnp.int32, (nq, w), 1) < 64
    for h in range(ATT_KV_HEADS):
        hcol = slice(w * h, w * (h + 1))
        qs = []
        for g in range(ATT_GROUPS):
            a = ATT_GROUPS * h + g
            piece = q_ref[:, w * (a // 2):w * (a // 2 + 1)]
            qs.append(jnp.where(low if a % 2 == 0 else jnp.logical_not(low), piece, jnp.zeros_like(piece)))
        s = _dot_nt(jnp.concatenate(qs, axis=0), k_ref[:, hcol])
        ps, dens = [], []
        for g in range(ATT_GROUPS):
            sg = s[nq * g:nq * (g + 1)]
            if bias is not None:
                sg = sg + bias
            snk = sink_ref[ATT_GROUPS * h + g] * LOG2E
            m = jnp.maximum(jnp.max(sg, axis=-1, keepdims=True), snk)
            p = jnp.exp2(sg - m)
            dens.append(jnp.sum(p, axis=-1, keepdims=True) + jnp.exp2(snk - m))
            ps.append(p.astype(BF16))
        o = _dot(jnp.concatenate(ps, axis=0), v_ref[:, hcol])
        for pair in range(ATT_GROUPS // 2):
            g0, g1 = 2 * pair, 2 * pair + 1
            o0 = o[nq * g0:nq * (g0 + 1)] / dens[g0]
            o1 = o[nq * g1:nq * (g1 + 1)] / dens[g1]
            col = (ATT_GROUPS * h) // 2 + pair
            o_ref[:, w * col:w * (col + 1)] = jnp.where(low, o0, o1).astype(BF16)


def _attn_prompt_kernel(sink_ref, q_ref, k_ref, v_ref, o_ref):
    _attend(q_ref, k_ref, v_ref, None, sink_ref, o_ref)


def _attn_sample_kernel(sink_ref, q_ref, kp_ref, km_ref, kn_ref, vp_ref, vm_ref, vn_ref,
                        kctx_ref, vctx_ref, o_ref, k_a, v_a, k_b, v_b):
    j = pl.program_id(1)
    nj = pl.num_programs(1)
    blk = ATT_BLOCK
    n_ctx = kctx_ref.shape[0]
    row = lax.broadcasted_iota(jnp.int32, (blk, blk), 0)
    col = lax.broadcasted_iota(jnp.int32, (blk, blk), 1)
    zero_blk, zero_ctx = jnp.zeros((blk, blk), F32), jnp.zeros((blk, n_ctx), F32)

    def make_bias(has_prev, has_next):
        prev = jnp.where(jnp.logical_and(col >= row, has_prev), 0.0, NEG_INF).astype(F32)
        nxt = jnp.where(jnp.logical_and(col <= row, has_next), 0.0, NEG_INF).astype(F32)
        return jnp.concatenate([prev, zero_blk, nxt, zero_ctx], axis=1)

    for (dst_a, dst_b), (p_ref, m_ref, n_ref) in (((k_a, k_b), (kp_ref, km_ref, kn_ref)),
                                                 ((v_a, v_b), (vp_ref, vm_ref, vn_ref))):
        dst_a[0:blk, :] = p_ref[...]
        dst_a[blk:3 * blk, :] = m_ref[...]
        dst_b[0:2 * blk, :] = m_ref[...]
        dst_b[2 * blk:3 * blk, :] = n_ref[...]

    @pl.when(j == 0)
    def _():
        low_ctx = lax.broadcasted_iota(jnp.int32, (n_ctx, V7X_LANES), 1) < 64
        for dsts, src in (((k_a, k_b), kctx_ref), ((v_a, v_b), vctx_ref)):
            for c in range(src.shape[1] // V7X_LANES):
                a, b = _dup_halves(src[:, V7X_LANES * c:V7X_LANES * (c + 1)], low_ctx)
                for dst in dsts:
                    dst[3 * blk:, V7X_LANES * (2 * c):V7X_LANES * (2 * c + 1)] = a.astype(BF16)
                    dst[3 * blk:, V7X_LANES * (2 * c + 1):V7X_LANES * (2 * c + 2)] = b.astype(BF16)

    _attend(q_ref.at[0:blk, :], k_a, v_a, make_bias(j > 0, True), sink_ref, o_ref.at[0:blk, :])
    _attend(q_ref.at[blk:2 * blk, :], k_b, v_b, make_bias(True, j < nj - 1), sink_ref,
            o_ref.at[blk:2 * blk, :])


def _ret_kernel(lg_ref, q_ref, k_ref, v_ref, g_ref, *rest, t_len, c_len, prompt):
    if prompt:
        gn_ref, o_ref, sfin_ref, obuf, sf, sb = rest
    else:
        s0_ref, gn_ref, o_ref, obuf, sf, sb = rest
    nc = t_len // c_len
    h = pl.program_id(1)
    lgf = lg_ref[0, h]
    lgb = lg_ref[1, h]
    row = lax.broadcasted_iota(jnp.int32, (c_len, c_len), 0).astype(F32)
    col = lax.broadcasted_iota(jnp.int32, (c_len, c_len), 1).astype(F32)
    diff = row - col
    decay_f = jnp.where(diff >= 0, jnp.exp(lgf * jnp.maximum(diff, 0.0)), 0.0)
    decay_b = jnp.where(diff < 0, jnp.exp(lgb * jnp.maximum(-diff, 0.0)), 0.0)
    idx = lax.broadcasted_iota(jnp.int32, (c_len, 1), 0).astype(F32)
    one = jnp.ones((1, 1), F32)
    wq_f, ws_f, gc_f = jnp.exp(lgf * (idx + 1.0)), jnp.exp(lgf * (c_len - 1.0 - idx)), jnp.exp(lgf * c_len * one)
    wq_b, ws_b, gc_b = jnp.exp(lgb * (c_len - idx)), jnp.exp(lgb * idx), jnp.exp(lgb * c_len * one)
    if prompt:
        sf[...] = jnp.zeros(sf.shape, F32)
        sb[...] = jnp.zeros(sb.shape, F32)
    else:
        sf[...] = s0_ref[0]
        sb[...] = s0_ref[1]

    def rows(c):
        start = c * c_len
        return pl.ds(start if isinstance(start, int) else pl.multiple_of(start, c_len), c_len)

    def chunk(sl, state, decay, wq, ws, gc):
        qc, kc, vc = q_ref[sl, :], k_ref[sl, :], v_ref[sl, :]
        s_prev = state[...]
        sc = (_dot_nt(qc, kc) * decay).astype(BF16)
        o = _dot(sc, vc) + _dot(qc, s_prev.astype(BF16)) * wq
        kw = (kc.astype(F32) * ws).astype(BF16)
        state[...] = gc * s_prev + _dot_tn(kw, vc)
        return o

    def fwd(sl):
        return chunk(sl, sf, decay_f, wq_f, ws_f, gc_f)

    def bwd(sl):
        return chunk(sl, sb, decay_b, wq_b, ws_b, gc_b)

    def finish(sl, o):
        mu = jnp.mean(o, axis=-1, keepdims=True)
        dev = o - mu
        var = jnp.mean(dev * dev, axis=-1, keepdims=True)
        on = dev * lax.rsqrt(var + RMS_EPS) * gn_ref[...]
        o_ref[sl, :] = (_silu(g_ref[sl, :].astype(F32)) * on).astype(BF16)

    half = nc // 2

    def first_half(i, carry):
        sl_f, sl_b = rows(i), rows(nc - 1 - i)
        obuf[sl_f, :] = fwd(sl_f)
        obuf[sl_b, :] = bwd(sl_b)
        return carry

    lax.fori_loop(0, half, first_half, 0)
    if nc % 2:
        sl = rows(half)
        finish(sl, fwd(sl) + bwd(sl))

    def second_half(i, carry):
        sl_f, sl_b = rows(i), rows(nc - 1 - i)
        finish(sl_f, fwd(sl_f) + obuf[sl_f, :])
        finish(sl_b, bwd(sl_b) + obuf[sl_b, :])
        return carry

    lax.fori_loop(half + nc % 2, nc, second_half, 0)
    if prompt:
        sfin_ref[0] = sf[...]
        sfin_ref[1] = sb[...]


def _softplus(x):
    return jnp.maximum(x, 0.0) + jnp.log1p(jnp.exp(-jnp.abs(x)))


def _sigmoid(x):
    return 0.5 * jnp.tanh(0.5 * x) + 0.5


def _lru_kernel(xr_ref, prev_ref, next_ref, act_ref, cw_ref, cb_ref, wg_ref, br_ref, bi_ref, lam_ref,
                h0_ref, y_ref, hfin_ref, hf_buf, xc_buf, a_s, u_s, hb_s, hc, *, nc):
    tc, d_rnn = xr_ref.shape
    bw = d_rnn // LRU_BLOCKS
    s = pl.program_id(1)
    is_b = s >= nc
    c = jnp.where(is_b, 2 * nc - 1 - s, s)
    base = pl.multiple_of(c * tc, tc)

    @pl.when(jnp.logical_not(is_b))
    def _():
        x = xr_ref[...]
        rowi = lax.broadcasted_iota(jnp.int32, (tc, d_rnn), 0)
        p1 = jnp.where(c > 0, prev_ref[V7X_SUBLANES - 1:V7X_SUBLANES, :], 0.0)
        p2 = jnp.where(c > 0, prev_ref[V7X_SUBLANES - 2:V7X_SUBLANES - 1, :], 0.0)
        n1 = jnp.where(c < nc - 1, next_ref[0:1, :], 0.0)
        xm1 = jnp.where(rowi == 0, p1, pltpu.roll(x, 1, 0))
        xm2 = jnp.where(rowi == 0, p2, jnp.where(rowi == 1, p1, pltpu.roll(x, 2, 0)))
        xp1 = jnp.where(rowi == tc - 1, n1, pltpu.roll(x, tc - 1, 0))
        xc_buf[pl.ds(base, tc), :] = (cw_ref[0:1, :] * xm2 + cw_ref[1:2, :] * xm1 + cw_ref[2:3, :] * x
                                      + cw_ref[3:4, :] * xp1 + cb_ref[...])

    xc = xc_buf[pl.ds(base, tc), :]
    xcb = xc.astype(BF16)
    sp = _softplus(-lam_ref[...])
    for n in range(LRU_BLOCKS):
        cs = slice(bw * n, bw * (n + 1))
        z = _dot(xcb[:, cs], wg_ref[n])
        r = _sigmoid(z[:, :bw] + br_ref[:, cs])
        i = _sigmoid(z[:, bw:] + bi_ref[:, cs])
        log_a = -LRU_C * r * sp[:, cs]
        a = jnp.exp(log_a)
        u = jnp.sqrt(-jnp.tanh(log_a) * (a * a + 1.0)) * (i * xc[:, cs])
        a_s[:, cs] = a
        u_s[:, cs] = u

    @pl.when(s == 0)
    def _():
        hc[...] = h0_ref[0:1, :]

    @pl.when(s == nc)
    def _():
        hc[...] = h0_ref[1:2, :]

    @pl.when(jnp.logical_not(is_b))
    def _():
        def step(t, h):
            h = a_s[pl.ds(t, 1), :] * h + u_s[pl.ds(t, 1), :]
            hf_buf[pl.ds(base + t, 1), :] = h
            return h
        h = lax.fori_loop(0, tc, step, hc[...], unroll=8)
        hc[...] = h

        @pl.when(s == nc - 1)
        def _():
            hfin_ref[0:1, :] = h

    @pl.when(is_b)
    def _():
        def step(k, h):
            t = tc - 1 - k
            h = a_s[pl.ds(t, 1), :] * h + u_s[pl.ds(t, 1), :]
            hb_s[pl.ds(t, 1), :] = h
            return h
        h = lax.fori_loop(0, tc, step, hc[...], unroll=8)
        hc[...] = h
        y_ref[...] = (act_ref[...] * (hf_buf[pl.ds(base, tc), :] + hb_s[...])).astype(BF16)

        @pl.when(s == 2 * nc - 1)
        def _():
            hfin_ref[1:2, :] = h


def _out_mlp_kernel(*refs, n_prompt, split_x, split_out):
    x, rest = _read_x(refs, n_prompt, split_x)
    yp_ref, ys_ref, mod_ref, gm_ref, wo_ref, wu_ref, wd_ref = rest[:7]
    d = x.shape[1]
    d_ff = wu_ref.shape[1]
    is_prompt = pl.program_id(0) < n_prompt
    y = jnp.where(is_prompt, yp_ref[...], ys_ref[...])
    x1 = x + mod_ref[:, 2 * d:3 * d] * _dot(y, wo_ref[...])
    hn = _modulate(x1, gm_ref[...], mod_ref[:, 3 * d:4 * d], mod_ref[:, 4 * d:5 * d]).astype(BF16)
    acc = jnp.zeros(x1.shape, F32)
    for c in range(d_ff // FF_CHUNK):
        hh = jnp.maximum(_dot(hn, wu_ref[:, FF_CHUNK * c:FF_CHUNK * (c + 1)]), 0.0)
        acc = acc + _dot((hh * hh).astype(BF16), wd_ref[FF_CHUNK * c:FF_CHUNK * (c + 1), :])
    out = x1 + mod_ref[:, 5 * d:6 * d] * acc
    if split_out:
        op_ref, os_ref = rest[7:]

        @pl.when(is_prompt)
        def _():
            op_ref[...] = out

        @pl.when(jnp.logical_not(is_prompt))
        def _():
            os_ref[...] = out
    else:
        rest[7][...] = out


class _Dims:
    def __init__(self, bp, lp, bs, ts, d):
        assert bp * lp == ts, "prompt tokens must fill exactly one segment"
        self.bp, self.lp, self.bs, self.ts, self.d = bp, lp, bs, ts, d
        self.nseg = 1 + bs
        self.n = self.nseg * ts
        self.tm = min(TOKEN_TILE, ts)
        assert ts % self.tm == 0 and ts % ATT_BLOCK == 0
        assert ts % RET_KERNEL_CHUNK == 0 and lp % min(RET_KERNEL_CHUNK, lp) == 0
        self.tps = ts // self.tm
        self.ntiles = self.n // self.tm


def _in_kernel(body, dm, split_x):
    return functools.partial(body, n_prompt=dm.tps, split_x=split_x)


def _attn_layer(dm, layer, slot, xs, norm_mix, mods, w_in, q_gain, k_gain, sink, cache_k, cache_v, rope):
    d, tm, tps, n = dm.d, dm.tm, dm.tps, dm.n
    split_x = len(xs) == 2
    hd = ATT_HEAD_DIM
    q_w, kv_w = ATT_HEADS * hd, ATT_KV_HEADS * hd
    gain = jnp.concatenate([jnp.tile(q_gain, ATT_HEADS), jnp.tile(k_gain, ATT_KV_HEADS)])[None, :]
    gw = q_w + kv_w
    assert (q_w + kv_w) % gw == 0 and gw % V7X_LANES == 0 and gw // hd <= V7X_LANES
    head_of_lane = np.arange(gw) // hd
    hsum = jnp.asarray(head_of_lane[:, None] == np.arange(V7X_LANES)[None, :], BF16)
    hexp = jnp.asarray(np.arange(V7X_LANES)[:, None] == head_of_lane[None, :], BF16)
    cos, sin = rope
    nt = dm.ntiles
    cur = lambda i: jnp.minimum(i, nt - 1)
    prev = lambda i: jnp.maximum(i - 1, 0)
    tok = lambda w: pl.BlockSpec((tm, w), lambda i: (prev(i), 0))
    q, k, v, kf, vf = pl.pallas_call(
        _in_kernel(_attn_in_kernel, dm, split_x),
        grid=(nt + 1,),
        in_specs=_token_specs(tm, tps, d, layer, split_x, cur) + [
            _resident(w_in.shape), _resident(gain.shape)] + _rope_specs(tm, tps, V7X_LANES, prev) + [
            _resident(hsum.shape), _resident(hexp.shape)],
        out_specs=[tok(q_w), tok(2 * kv_w), tok(2 * kv_w), tok(kv_w), tok(kv_w)],
        out_shape=[jax.ShapeDtypeStruct((n, q_w), BF16), jax.ShapeDtypeStruct((n, 2 * kv_w), BF16),
                   jax.ShapeDtypeStruct((n, 2 * kv_w), BF16), jax.ShapeDtypeStruct((n, kv_w), F32),
                   jax.ShapeDtypeStruct((n, kv_w), F32)],
        scratch_shapes=[pltpu.VMEM((2, tm, q_w + 2 * kv_w), F32)],
        compiler_params=_cparams(("arbitrary",)),
        name="attn_in",
    )(*xs, norm_mix, mods, w_in, gain, cos, sin, hsum, hexp)
    kf, vf = kf[:dm.ts], vf[:dm.ts]

    smem = pl.BlockSpec(memory_space=pltpu.SMEM)
    lp, bp, bs, ts = dm.lp, dm.bp, dm.bs, dm.ts
    y_p = pl.pallas_call(
        _attn_prompt_kernel,
        grid=(bp,),
        in_specs=[smem, pl.BlockSpec((lp, q_w), lambda b: (b, 0)),
                  pl.BlockSpec((lp, 2 * kv_w), lambda b: (b, 0)),
                  pl.BlockSpec((lp, 2 * kv_w), lambda b: (b, 0))],
        out_specs=pl.BlockSpec((lp, q_w), lambda b: (b, 0)),
        out_shape=jax.ShapeDtypeStruct((ts, q_w), BF16),
        compiler_params=_cparams(("arbitrary",)),
        name="attn_prompt",
    )(sink, q, k, v)

    blk = ATT_BLOCK
    nb = ts // blk
    assert nb % 2 == 0
    nj = nb // 2
    mid = lambda b, j: ((1 + b) * nj + j, 0)
    prv = lambda b, j: ((1 + b) * nb + jnp.maximum(2 * j - 1, 0), 0)
    nxt = lambda b, j: ((1 + b) * nb + jnp.minimum(2 * j + 2, nb - 1), 0)
    past = cache_k.shape[2]
    nk = 3 * blk + past
    ctx_spec = pl.BlockSpec((None, None, past, kv_w), lambda b, j: (b, slot, 0, 0))
    edge_spec = lambda f: pl.BlockSpec((blk, 2 * kv_w), f)
    mid_spec = pl.BlockSpec((2 * blk, 2 * kv_w), mid)
    key_scratch = pltpu.VMEM((nk, 2 * kv_w), BF16)
    y_s = pl.pallas_call(
        _attn_sample_kernel,
        grid=(bs, nj),
        in_specs=[smem, pl.BlockSpec((2 * blk, q_w), mid), edge_spec(prv), mid_spec, edge_spec(nxt),
                  edge_spec(prv), mid_spec, edge_spec(nxt), ctx_spec, ctx_spec],
        out_specs=pl.BlockSpec((2 * blk, q_w), lambda b, j: (b * nj + j, 0)),
        out_shape=jax.ShapeDtypeStruct((bs * ts, q_w), BF16),
        scratch_shapes=[key_scratch] * 4,
        compiler_params=_cparams(("arbitrary", "arbitrary")),
        name="attn_sample",
    )(sink, q, k, k, k, v, v, v, cache_k, cache_v)
    return y_p, y_s, kf, vf


def _ret_layer(dm, layer, slot, xs, norm_mix, mods, w_in, gn_gain, log_decay, state, rope):
    d, tm, tps, n = dm.d, dm.tm, dm.tps, dm.n
    split_x = len(xs) == 2
    e1, ev = RET_HEADS * RET_DK, RET_HEADS * RET_DV
    cos, sin = rope
    tok = lambda w: pl.BlockSpec((tm, w), lambda i: (i, 0))
    q, k, v, g = pl.pallas_call(
        _in_kernel(_ret_in_kernel, dm, split_x),
        grid=(dm.ntiles,),
        in_specs=(_token_specs(tm, tps, d, layer, split_x) + [_resident(w_in.shape)]
                  + _rope_specs(tm, tps, RET_DK)),
        out_specs=[tok(e1), tok(e1), tok(ev), tok(ev)],
        out_shape=[jax.ShapeDtypeStruct((n, e1), BF16), jax.ShapeDtypeStruct((n, e1), BF16),
                   jax.ShapeDtypeStruct((n, ev), BF16), jax.ShapeDtypeStruct((n, ev), BF16)],
        compiler_params=_cparams(("arbitrary",)),
        name="ret_in",
    )(*xs, norm_mix, mods, w_in, cos, sin)

    smem = pl.BlockSpec(memory_space=pltpu.SMEM)
    gn = gn_gain[None, :]
    lp, bp, bs, ts = dm.lp, dm.bp, dm.bs, dm.ts
    state_spec = lambda f: pl.BlockSpec((None, None, 2, None, RET_DK, RET_DV), f)

    def call(t_len, nbatch, seg0, prompt, name):
        rows = lambda w: pl.BlockSpec((t_len, w), lambda b, h: (seg0 + b, h))
        in_specs = [smem, rows(RET_DK), rows(RET_DK), rows(RET_DV), rows(RET_DV)]
        args = [log_decay, q, k, v, g]
        out_shape = [jax.ShapeDtypeStruct((nbatch * t_len, ev), BF16)]
        out_specs = [pl.BlockSpec((t_len, RET_DV), lambda b, h: (b, h))]
        if prompt:
            out_shape.append(jax.ShapeDtypeStruct((nbatch, 1, 2, RET_HEADS, RET_DK, RET_DV), F32))
            out_specs.append(state_spec(lambda b, h: (b, 0, 0, h, 0, 0)))
        else:
            in_specs.append(state_spec(lambda b, h: (b, slot, 0, h, 0, 0)))
            args.append(state)
        in_specs.append(pl.BlockSpec((1, RET_DV), lambda b, h: (0, h)))
        args.append(gn)
        return pl.pallas_call(
            functools.partial(_ret_kernel, t_len=t_len, c_len=min(RET_KERNEL_CHUNK, t_len), prompt=prompt),
            grid=(nbatch, RET_HEADS),
            in_specs=in_specs, out_specs=out_specs, out_shape=out_shape,
            scratch_shapes=[pltpu.VMEM((t_len, RET_DV), F32), pltpu.VMEM((RET_DK, RET_DV), F32),
                            pltpu.VMEM((RET_DK, RET_DV), F32)],
            compiler_params=_cparams(("arbitrary", "arbitrary")),
            name=name,
        )(*args)

    y_p, s_new = call(lp, bp, 0, True, "ret_prompt")
    (y_s,) = call(ts, bs, 1, False, "ret_sample")
    return y_p, y_s, s_new


def _lru_layer(dm, layer, slot, xs, norm_mix, mods, w_in, conv_w, conv_b, w_r, b_r, w_i, b_i, lam, state):
    d, tm, tps, n = dm.d, dm.tm, dm.tps, dm.n
    split_x = len(xs) == 2
    d_rnn = w_in.shape[1] // 2
    tok = lambda w: pl.BlockSpec((tm, w), lambda i: (i, 0))
    act, xr = pl.pallas_call(
        _in_kernel(_lru_in_kernel, dm, split_x),
        grid=(dm.ntiles,),
        in_specs=_token_specs(tm, tps, d, layer, split_x) + [_resident(w_in.shape)],
        out_specs=[tok(d_rnn), tok(d_rnn)],
        out_shape=[jax.ShapeDtypeStruct((n, d_rnn), F32), jax.ShapeDtypeStruct((n, d_rnn), F32)],
        compiler_params=_cparams(("arbitrary",)),
        name="lru_in",
    )(*xs, norm_mix, mods, w_in)

    wg = jnp.concatenate([w_r, w_i], axis=-1).astype(BF16)
    vec = lambda a: a[:, None, :]
    lp, bp, bs, ts = dm.lp, dm.bp, dm.bs, dm.ts
    sub = V7X_SUBLANES

    def call(t_len, nbatch, row0, h0, h0_slot, name):
        tc = t_len if t_len <= 256 else min(512, t_len // 2)
        nc = t_len // tc
        cb = row0 // tc
        early = lambda s: jnp.minimum(s, nc - 1)
        late = lambda s: jnp.where(s >= nc, 2 * nc - 1 - s, nc - 1)
        dirv = lambda s: jnp.where(s >= nc, 1, 0)
        r8 = tc // sub
        first8 = lambda b: (row0 + b * t_len) // sub
        prev8 = lambda b, s: (jnp.maximum(first8(b) + early(s) * r8 - 1, first8(b)), 0)
        next8 = lambda b, s: (jnp.minimum(first8(b) + (early(s) + 1) * r8, first8(b) + t_len // sub - 1), 0)
        dspec = lambda shape: pl.BlockSpec((None,) + shape, lambda b, s: (dirv(s),) + (0,) * len(shape))
        return pl.pallas_call(
            functools.partial(_lru_kernel, nc=nc),
            grid=(nbatch, 2 * nc),
            in_specs=[pl.BlockSpec((tc, d_rnn), lambda b, s: (cb + b * nc + early(s), 0)),
                      pl.BlockSpec((sub, d_rnn), prev8), pl.BlockSpec((sub, d_rnn), next8),
                      pl.BlockSpec((tc, d_rnn), lambda b, s: (cb + b * nc + late(s), 0)),
                      pl.BlockSpec((CONV_W, d_rnn), lambda b, s: (0, 0)),
                      pl.BlockSpec((1, d_rnn), lambda b, s: (0, 0)),
                      dspec(wg.shape[1:]), dspec((1, d_rnn)), dspec((1, d_rnn)), dspec((1, d_rnn)),
                      pl.BlockSpec((None, None, 2, d_rnn), lambda b, s: (b, h0_slot, 0, 0))],
            out_specs=[pl.BlockSpec((tc, d_rnn), lambda b, s: (b * nc + late(s), 0)),
                       pl.BlockSpec((None, 2, d_rnn), lambda b, s: (b, 0, 0))],
            out_shape=[jax.ShapeDtypeStruct((nbatch * t_len, d_rnn), BF16),
                       jax.ShapeDtypeStruct((nbatch, 2, d_rnn), F32)],
            scratch_shapes=[pltpu.VMEM((t_len, d_rnn), F32), pltpu.VMEM((t_len, d_rnn), F32),
                            pltpu.VMEM((tc, d_rnn), F32), pltpu.VMEM((tc, d_rnn), F32),
                            pltpu.VMEM((tc, d_rnn), F32), pltpu.VMEM((1, d_rnn), F32)],
            compiler_params=_cparams(("arbitrary", "arbitrary")),
            name=name,
        )(xr, xr, xr, act, conv_w, conv_b[None, :], wg, vec(b_r), vec(b_i), vec(lam), h0)

    y_p, h_new = call(lp, bp, 0, jnp.zeros((bp, 1, 2, d_rnn), F32), 0, "lru_prompt")
    y_s, _ = call(ts, bs, ts, state, slot, "lru_sample")
    return y_p, y_s, h_new


def _out_mlp(dm, layer, xs, y_p, y_s, norm_mlp, mods, w_out, w_up, w_down, split_out):
    d, tm, tps, n = dm.d, dm.tm, dm.tps, dm.n
    split_x = len(xs) == 2
    din = w_out.shape[0]
    first = lambda i: (jnp.minimum(i, tps - 1), 0)
    second = lambda i: (jnp.maximum(i - tps, 0), 0)
    if split_out:
        out_specs = [pl.BlockSpec((tm, d), first), pl.BlockSpec((tm, d), second)]
        out_shape = [jax.ShapeDtypeStruct((dm.ts, d), F32), jax.ShapeDtypeStruct((n - dm.ts, d), F32)]
    else:
        out_specs = pl.BlockSpec((tm, d), lambda i: (i, 0))
        out_shape = jax.ShapeDtypeStruct((n, d), F32)
    return pl.pallas_call(
        functools.partial(_out_mlp_kernel, n_prompt=tps, split_x=split_x, split_out=split_out),
        grid=(dm.ntiles,),
        in_specs=_x_specs(tm, tps, d, split_x) + [
            pl.BlockSpec((tm, din), first), pl.BlockSpec((tm, din), second),
            pl.BlockSpec((None, None, 1, N_MOD * d), lambda i: (layer, i // tps, 0, 0)),
            pl.BlockSpec((None, 1, d), lambda i: (layer, 0, 0)),
            _resident(w_out.shape), _resident(w_up.shape), _resident(w_down.shape)],
        out_specs=out_specs, out_shape=out_shape,
        compiler_params=_cparams(("arbitrary",)),
        name="out_mlp",
    )(*xs, y_p, y_s, mods, norm_mlp, w_out, w_up, w_down)


def kernel(x_prompt, x_sample, cache_attn_k, cache_attn_v, state_ret, state_lru, c, c_ctx, norm_mix, norm_mlp, w_ada, b_ada, w_up, w_down, attn_w_in, attn_w_out, attn_q_gain, attn_k_gain, attn_sink, ret_w_in, ret_w_out, ret_gn_gain, ret_log_decay, lru_w_in, lru_conv_w, lru_conv_b, lru_w_r, lru_b_r, lru_w_i, lru_b_i, lru_lambda, lru_w_out):
    bp, lp, d = x_prompt.shape
    bs, ts, _ = x_sample.shape
    dm = _Dims(bp, lp, bs, ts, d)
    depth = w_ada.shape[0]
    assert dm.nseg <= MOD_ROWS

    xs = (x_prompt.reshape(ts, d), x_sample.reshape(bs * ts, d))
    cvec = jnp.concatenate([c_ctx[None, :], c, jnp.zeros((MOD_ROWS - dm.nseg, d), F32)], axis=0)
    mods = _modulation(cvec, w_ada, b_ada)[:, :dm.nseg].reshape(depth, dm.nseg, 1, N_MOD * d)
    nmix = norm_mix[:, None, :]
    nmlp = norm_mlp[:, None, :]

    rope_attn = tuple(jnp.tile(t, (1, 1, 2)) for t in _rope_tables(ts, ATT_HEAD_DIM // 2))
    rope_ret = _rope_tables(ts, RET_DK // 2)
    kv_w = ATT_KV_HEADS * ATT_HEAD_DIM
    past = cache_attn_k.shape[2]
    ck = cache_attn_k.reshape(bs, -1, past, kv_w)
    cv = cache_attn_v.reshape(bs, -1, past, kv_w)

    new_k, new_v, new_ret, new_lru = [], [], [], []
    for layer in range(depth):
        kind, slot = layer % 3, layer // 3
        if kind == 0:
            y_p, y_s, kf, vf = _attn_layer(dm, layer, slot, xs, nmix, mods, attn_w_in[slot].astype(BF16),
                                           attn_q_gain[slot], attn_k_gain[slot], attn_sink[slot],
                                           ck, cv, rope_attn)
            new_k.append(kf.reshape(bp, lp, ATT_KV_HEADS, ATT_HEAD_DIM))
            new_v.append(vf.reshape(bp, lp, ATT_KV_HEADS, ATT_HEAD_DIM))
            w_out = attn_w_out[slot]
        elif kind == 1:
            y_p, y_s, s_new = _ret_layer(dm, layer, slot, xs, nmix, mods, ret_w_in[slot].astype(BF16),
                                         ret_gn_gain[slot], ret_log_decay[slot], state_ret, rope_ret)
            new_ret.append(s_new)
            w_out = ret_w_out[slot]
        else:
            y_p, y_s, h_new = _lru_layer(dm, layer, slot, xs, nmix, mods, lru_w_in[slot].astype(BF16),
                                         lru_conv_w[slot], lru_conv_b[slot], lru_w_r[slot], lru_b_r[slot],
                                         lru_w_i[slot], lru_b_i[slot], lru_lambda[slot], state_lru)
            new_lru.append(h_new)
            w_out = lru_w_out[slot]
        last = layer == depth - 1
        out = _out_mlp(dm, layer, xs, y_p, y_s, nmlp, mods, w_out.astype(BF16),
                       w_up[layer].astype(BF16), w_down[layer].astype(BF16), split_out=last)
        xs = tuple(out) if last else (out,)

    y_prompt = xs[0].reshape(bp, lp, d)
    y_sample = xs[1].reshape(bs, ts, d)
    return (y_prompt, y_sample, jnp.stack(new_k, axis=1), jnp.stack(new_v, axis=1),
            jnp.concatenate(new_ret, axis=1), jnp.stack(new_lru, axis=1))
```

```python
import functools

import jax
import jax.numpy as jnp
import numpy as np
from jax import lax
from jax.experimental import pallas as pl
from jax.experimental.pallas import tpu as pltpu

F32 = jnp.float32
BF16 = jnp.bfloat16

N_MOD = 6
RMS_EPS = 1e-6
ROPE_BASE = 10000.0
NEG_INF = -1e30
LOG2E = 1.4426950408889634
GRID_W = 64
ATT_HEADS = 16
ATT_KV_HEADS = 4
ATT_HEAD_DIM = 64
ATT_GROUPS = ATT_HEADS // ATT_KV_HEADS
ATT_BLOCK = 128
RET_HEADS = 4
RET_DK = 256
RET_DV = 512
LRU_BLOCKS = 8
CONV_W = 4
LRU_C = 8.0

V7X_LANES = 128
V7X_SUBLANES = 8
V7X_VMEM_BYTES = 64 * 1024 * 1024
VMEM_LIMIT = V7X_VMEM_BYTES - 8 * 1024 * 1024

TOKEN_TILE = 512
MOD_ROWS = 16
FF_CHUNK = 1024
RET_KERNEL_CHUNK = 256


def _cparams(sem):
    return pltpu.CompilerParams(dimension_semantics=sem, vmem_limit_bytes=VMEM_LIMIT)


def _resident(shape):
    zeros = (0,) * len(shape)
    return pl.BlockSpec(shape, lambda *_: zeros, pipeline_mode=pl.Buffered(1))


def _layer_resident(shape, layer):
    zeros = (0,) * (len(shape) - 1)
    return pl.BlockSpec((None,) + tuple(shape[1:]), lambda *_: (layer,) + zeros, pipeline_mode=pl.Buffered(1))


def _dot(a, b):
    return jnp.dot(a, b, preferred_element_type=F32)


def _dot_nt(a, b):
    return lax.dot_general(a, b, (((1,), (1,)), ((), ())), preferred_element_type=F32)


def _dot_tn(a, b):
    return lax.dot_general(a, b, (((0,), (0,)), ((), ())), preferred_element_type=F32)


def _silu(x):
    return x * jax.nn.sigmoid(x)


def _modulate(x, g, shift, scale):
    ms = jnp.mean(x * x, axis=-1, keepdims=True)
    y = x * lax.rsqrt(ms + RMS_EPS) * g
    return y * (1.0 + scale) + shift


def _split_bf16(x):
    hi = x.astype(BF16)
    lo = (x - hi.astype(F32)).astype(BF16)
    return hi, lo


def _read_x(refs, n_prompt, split_x):
    if split_x:
        x = jnp.where(pl.program_id(0) < n_prompt, refs[0][...], refs[1][...])
        return x, refs[2:]
    return refs[0][...], refs[1:]


def _mod_kernel(c_ref, w_ref, b_ref, o_ref):
    s = _silu(c_ref[...]).astype(BF16)
    o_ref[...] = _dot(s, w_ref[...].astype(BF16)) + b_ref[...]


def _modulation(cvec, w_ada, b_ada):
    depth, d, w = w_ada.shape
    tn = 1536
    return pl.pallas_call(
        _mod_kernel,
        grid=(depth, w // tn),
        in_specs=[pl.BlockSpec((MOD_ROWS, d), lambda l, j: (0, 0)),
                  pl.BlockSpec((None, d, tn), lambda l, j: (l, 0, j)),
                  pl.BlockSpec((None, 1, tn), lambda l, j: (l, 0, j))],
        out_specs=pl.BlockSpec((None, MOD_ROWS, tn), lambda l, j: (l, 0, j)),
        out_shape=jax.ShapeDtypeStruct((depth, MOD_ROWS, w), F32),
        compiler_params=_cparams(("arbitrary", "arbitrary")),
        name="modulation",
    )(cvec, w_ada, b_ada.reshape(depth, 1, w))


def _rope_tables(t_len, sect):
    half = sect // 2
    inv = ROPE_BASE ** (-jnp.arange(half, dtype=F32) / half)
    t = jnp.arange(t_len, dtype=jnp.int32)

    def one(pos):
        ang = pos.astype(F32)[:, None] * inv[None, :]
        c, s = jnp.cos(ang), jnp.sin(ang)
        return jnp.concatenate([c, c], axis=1), jnp.concatenate([-s, s], axis=1)

    cr, sr = one(t // GRID_W)
    cc, sc = one(t % GRID_W)
    cos = jnp.concatenate([cr, cc], axis=1)
    sin = jnp.concatenate([sr, sc], axis=1)
    return (jnp.stack([jnp.ones_like(cos), cos]), jnp.stack([jnp.zeros_like(sin), sin]))


def _dup_halves(x, lane_low):
    r = pltpu.roll(x, 64, 1)
    return jnp.where(lane_low, x, r), jnp.where(lane_low, r, x)


def _attn_in_kernel(*refs, n_prompt, split_x):
    x, rest = _read_x(refs, n_prompt, split_x)
    (g_ref, mod_ref, w_ref, gain_ref, cos_ref, sin_ref, hsum_ref, hexp_ref,
     q_out, k_out, v_out, kf_out, vf_out, proj_s) = rest
    tm, d = x.shape
    i = pl.program_id(0)
    qk_w = hsum_ref.shape[0]
    n_q = ATT_HEADS * ATT_HEAD_DIM // V7X_LANES

    @pl.when(i == 0)
    def _():
        proj_s[1] = jnp.zeros(proj_s.shape[1:], F32)

    def epilogue(proj):
        lane = lax.broadcasted_iota(jnp.int32, (tm, V7X_LANES), 1)
        first = (lane % 32) < 16
        low = lane < 64
        cos = cos_ref[...]
        sin = sin_ref[...]
        qk = proj[:, :qk_w]
        ssum = _dot((qk * qk).astype(BF16), hsum_ref[...])
        r_hi, r_lo = _split_bf16(lax.rsqrt(ssum * (1.0 / ATT_HEAD_DIM) + RMS_EPS))
        qkn = qk * (_dot(r_hi, hexp_ref[...]) + _dot(r_lo, hexp_ref[...])) * gain_ref[...]
        for j in range(qk_w // V7X_LANES):
            xj = qkn[:, V7X_LANES * j:V7X_LANES * (j + 1)]
            rot = jnp.where(first, pltpu.roll(xj, V7X_LANES - 16, 1), pltpu.roll(xj, 16, 1))
            yj = xj * cos + rot * sin
            if j < n_q:
                q_out[:, V7X_LANES * j:V7X_LANES * (j + 1)] = (yj * (ATT_HEAD_DIM ** -0.5 * LOG2E)).astype(BF16)
            else:
                jj = j - n_q
                kf_out[:, V7X_LANES * jj:V7X_LANES * (jj + 1)] = yj
                a, b = _dup_halves(yj, low)
                k_out[:, V7X_LANES * (2 * jj):V7X_LANES * (2 * jj + 1)] = a.astype(BF16)
                k_out[:, V7X_LANES * (2 * jj + 1):V7X_LANES * (2 * jj + 2)] = b.astype(BF16)
        v = proj[:, qk_w:]
        vf_out[...] = v
        for jj in range(v.shape[1] // V7X_LANES):
            a, b = _dup_halves(v[:, V7X_LANES * jj:V7X_LANES * (jj + 1)], low)
            v_out[:, V7X_LANES * (2 * jj):V7X_LANES * (2 * jj + 1)] = a.astype(BF16)
            v_out[:, V7X_LANES * (2 * jj + 1):V7X_LANES * (2 * jj + 2)] = b.astype(BF16)

    def step(slot):
        epilogue(proj_s[1 - slot])
        h = _modulate(x, g_ref[...], mod_ref[:, 0:d], mod_ref[:, d:2 * d]).astype(BF16)
        proj_s[slot] = _dot(h, w_ref[...])

    @pl.when(i % 2 == 0)
    def _():
        step(0)

    @pl.when(i % 2 == 1)
    def _():
        step(1)


def _ret_in_kernel(*refs, n_prompt, split_x):
    x, (g_ref, mod_ref, w_ref, cos_ref, sin_ref, q_out, k_out, v_out, g_out) = _read_x(refs, n_prompt, split_x)
    d = x.shape[1]
    e1 = RET_HEADS * RET_DK
    e3 = 2 * e1 + RET_HEADS * RET_DV
    h = _modulate(x, g_ref[...], mod_ref[:, 0:d], mod_ref[:, d:2 * d]).astype(BF16)
    n_chunks = e1 // V7X_LANES
    for sec, out in ((0, q_out), (1, k_out)):
        p = _dot(h, w_ref[:, sec * e1:(sec + 1) * e1])
        for j in range(n_chunks):
            s = j % 2
            xj = p[:, V7X_LANES * j:V7X_LANES * (j + 1)]
            yj = (xj * cos_ref[:, V7X_LANES * s:V7X_LANES * (s + 1)]
                  + pltpu.roll(xj, 64, 1) * sin_ref[:, V7X_LANES * s:V7X_LANES * (s + 1)])
            if sec == 1:
                yj = yj * (RET_DK ** -0.5)
            out[:, V7X_LANES * j:V7X_LANES * (j + 1)] = yj.astype(BF16)
    v_out[...] = _dot(h, w_ref[:, 2 * e1:e3]).astype(BF16)
    g_out[...] = _silu(_dot(h, w_ref[:, e3:])).astype(BF16)


def _lru_in_kernel(*refs, n_prompt, split_x):
    x, (g_ref, mod_ref, w_ref, act_out, xr_out) = _read_x(refs, n_prompt, split_x)
    d = x.shape[1]
    d_rnn = act_out.shape[1]
    h = _modulate(x, g_ref[...], mod_ref[:, 0:d], mod_ref[:, d:2 * d]).astype(BF16)
    act_out[...] = jax.nn.gelu(_dot(h, w_ref[:, :d_rnn]), approximate=True)
    xr_out[...] = _dot(h, w_ref[:, d_rnn:])


def _x_specs(tm, tps, d, split_x, tile=lambda i: i):
    if split_x:
        return [pl.BlockSpec((tm, d), lambda i: (jnp.minimum(tile(i), tps - 1), 0)),
                pl.BlockSpec((tm, d), lambda i: (jnp.maximum(tile(i) - tps, 0), 0))]
    return [pl.BlockSpec((tm, d), lambda i: (tile(i), 0))]


def _token_specs(tm, tps, d, layer, split_x, tile=lambda i: i):
    return _x_specs(tm, tps, d, split_x, tile) + [
        pl.BlockSpec((None, 1, d), lambda i: (layer, 0, 0)),
        pl.BlockSpec((None, None, 1, N_MOD * d), lambda i: (layer, tile(i) // tps, 0, 0))]


def _rope_specs(tm, tps, width, tile=lambda i: i):
    spec = pl.BlockSpec((None, tm, width), lambda i: (jnp.minimum(tile(i) // tps, 1), tile(i) % tps, 0))
    return [spec, spec]


def _attend(q_ref, k_ref, v_ref, bias, sink_ref, o_ref):
    nq = q_ref.shape[0]
    w = V7X_LANES
    low = lax.broadcasted_iota(jnp.int32, (nq, w), 1) < 64
    for h in range(ATT_KV_HEADS):
        hcol = slice(w * h, w * (h + 1))
        qs = []
        for g in range(ATT_GROUPS):
            a = ATT_GROUPS * h + g
            piece = q_ref[:, w * (a // 2):w * (a // 2 + 1)]
            qs.append(jnp.where(low if a % 2 == 0 else jnp.logical_not(low), piece, jnp.zeros_like(piece)))
        s = _dot_nt(jnp.concatenate(qs, axis=0), k_ref[:, hcol])
        ps, dens = [], []
        for g in range(ATT_GROUPS):
            sg = s[nq * g:nq * (g + 1)]
            if bias is not None:
                sg = sg + bias
            snk = sink_ref[ATT_GROUPS * h + g] * LOG2E
            m = jnp.maximum(jnp.max(sg, axis=-1, keepdims=True), snk)
            p = jnp.exp2(sg - m)
            dens.append(jnp.sum(p, axis=-1, keepdims=True) + jnp.exp2(snk - m))
            ps.append(p.astype(BF16))
        o = _dot(jnp.concatenate(ps, axis=0), v_ref[:, hcol])
        for pair in range(ATT_GROUPS // 2):
            g0, g1 = 2 * pair, 2 * pair + 1
            o0 = o[nq * g0:nq * (g0 + 1)] / dens[g0]
            o1 = o[nq * g1:nq * (g1 + 1)] / dens[g1]
            col = (ATT_GROUPS * h) // 2 + pair
            o_ref[:, w * col:w * (col + 1)] = jnp.where(low, o0, o1).astype(BF16)


def _attn_prompt_kernel(sink_ref, q_ref, k_ref, v_ref, o_ref):
    _attend(q_ref, k_ref, v_ref, None, sink_ref, o_ref)


def _attn_sample_kernel(sink_ref, q_ref, kp_ref, kc_ref, kn_ref, vp_ref, vc_ref, vn_ref,
                        kctx_ref, vctx_ref, o_ref, kall, vall, s_buf, p_buf, den_buf, *, nb, nblocks):
    i = pl.program_id(0)
    blk, w = ATT_BLOCK, V7X_LANES
    nq = blk
    n_ctx = kctx_ref.shape[0]
    j1 = jnp.minimum(i, nblocks - 1) % nb
    j2 = jnp.clip(i - 1, 0, nblocks - 1) % nb
    j3 = jnp.clip(i - 2, 0, nblocks - 1) % nb

    @pl.when(i == 0)
    def _():
        s_buf[...] = jnp.zeros(s_buf.shape, F32)
        p_buf[...] = jnp.zeros(p_buf.shape, BF16)
        den_buf[...] = jnp.ones(den_buf.shape, F32)

    def load_ctx(dst, src):
        low_ctx = lax.broadcasted_iota(jnp.int32, (n_ctx, w), 1) < 64
        for c in range(src.shape[1] // w):
            a, b = _dup_halves(src[:, w * c:w * (c + 1)], low_ctx)
            dst[3 * blk:, w * (2 * c):w * (2 * c + 1)] = a.astype(BF16)
            dst[3 * blk:, w * (2 * c + 1):w * (2 * c + 2)] = b.astype(BF16)

    @pl.when(j1 == 0)
    def _():
        load_ctx(kall, kctx_ref)

    @pl.when(j3 == 0)
    def _():
        load_ctx(vall, vctx_ref)

    def step(slot):
        low = lax.broadcasted_iota(jnp.int32, (nq, w), 1) < 64
        for r, src in enumerate((vp_ref, vc_ref, vn_ref)):
            vall[blk * r:blk * (r + 1), :] = src[...]
        for h in range(ATT_KV_HEADS):
            o = _dot(p_buf[1 - slot, h], vall[:, w * h:w * (h + 1)])
            for pair in range(ATT_GROUPS // 2):
                g0, g1 = 2 * pair, 2 * pair + 1
                o0 = o[nq * g0:nq * (g0 + 1)] / den_buf[1 - slot, h, nq * g0:nq * (g0 + 1), :]
                o1 = o[nq * g1:nq * (g1 + 1)] / den_buf[1 - slot, h, nq * g1:nq * (g1 + 1), :]
                col = (ATT_GROUPS * h) // 2 + pair
                o_ref[:, w * col:w * (col + 1)] = jnp.where(low, o0, o1).astype(BF16)
        row = lax.broadcasted_iota(jnp.int32, (blk, blk), 0)
        colm = lax.broadcasted_iota(jnp.int32, (blk, blk), 1)
        bias_prev = jnp.where(jnp.logical_and(colm >= row, j2 > 0), 0.0, NEG_INF).astype(F32)
        bias_next = jnp.where(jnp.logical_and(colm <= row, j2 < nb - 1), 0.0, NEG_INF).astype(F32)
        bias = jnp.concatenate([bias_prev, jnp.zeros((blk, blk), F32), bias_next,
                                jnp.zeros((blk, n_ctx), F32)], axis=1)
        for h in range(ATT_KV_HEADS):
            for g in range(ATT_GROUPS):
                rows = slice(nq * g, nq * (g + 1))
                sg = s_buf[1 - slot, h, rows, :] + bias
                snk = sink_ref[ATT_GROUPS * h + g] * LOG2E
                m = jnp.maximum(jnp.max(sg, axis=-1, keepdims=True), snk)
                p = jnp.exp2(sg - m)
                den = jnp.sum(p, axis=-1, keepdims=True) + jnp.exp2(snk - m)
                p_buf[slot, h, rows, :] = p.astype(BF16)
                den_buf[slot, h, rows, :] = jnp.broadcast_to(den, (nq, w))
        for r, src in enumerate((kp_ref, kc_ref, kn_ref)):
            kall[blk * r:blk * (r + 1), :] = src[...]
        for h in range(ATT_KV_HEADS):
            qs = []
            for g in range(ATT_GROUPS):
                a = ATT_GROUPS * h + g
                piece = q_ref[:, w * (a // 2):w * (a // 2 + 1)]
                qs.append(jnp.where(low if a % 2 == 0 else jnp.logical_not(low), piece, jnp.zeros_like(piece)))
            s_buf[slot, h] = _dot_nt(jnp.concatenate(qs, axis=0), kall[:, w * h:w * (h + 1)])

    @pl.when(i % 2 == 0)
    def _():
        step(0)

    @pl.when(i % 2 == 1)
    def _():
        step(1)


def _ret_kernel(lg_ref, q_ref, k_ref, v_ref, g_ref, *rest, t_len, c_len, prompt):
    if prompt:
        gn_ref, o_ref, sfin_ref, obuf, sf, sb = rest
    else:
        s0_ref, gn_ref, o_ref, obuf, sf, sb = rest
    nc = t_len // c_len
    h = pl.program_id(1)
    lgf = lg_ref[0, h]
    lgb = lg_ref[1, h]
    row = lax.broadcasted_iota(jnp.int32, (c_len, c_len), 0).astype(F32)
    col = lax.broadcasted_iota(jnp.int32, (c_len, c_len), 1).astype(F32)
    diff = row - col
    decay_f = jnp.where(diff >= 0, jnp.exp(lgf * jnp.maximum(diff, 0.0)), 0.0)
    decay_b = jnp.where(diff < 0, jnp.exp(lgb * jnp.maximum(-diff, 0.0)), 0.0)
    idx = lax.broadcasted_iota(jnp.int32, (c_len, 1), 0).astype(F32)
    one = jnp.ones((1, 1), F32)
    wq_f, ws_f, gc_f = jnp.exp(lgf * (idx + 1.0)), jnp.exp(lgf * (c_len - 1.0 - idx)), jnp.exp(lgf * c_len * one)
    wq_b, ws_b, gc_b = jnp.exp(lgb * (c_len - idx)), jnp.exp(lgb * idx), jnp.exp(lgb * c_len * one)
    if prompt:
        sf[...] = jnp.zeros(sf.shape, F32)
        sb[...] = jnp.zeros(sb.shape, F32)
    else:
        sf[...] = s0_ref[0]
        sb[...] = s0_ref[1]

    def rows(c):
        start = c * c_len
        return pl.ds(start if isinstance(start, int) else pl.multiple_of(start, c_len), c_len)

    def chunk(sl, state, decay, wq, ws, gc):
        qc, kc, vc = q_ref[sl, :], k_ref[sl, :], v_ref[sl, :]
        s_prev = state[...]
        sc = (_dot_nt(qc, kc) * decay).astype(BF16)
        o = _dot(sc, vc) + _dot(qc, s_prev.astype(BF16)) * wq
        kw = (kc.astype(F32) * ws).astype(BF16)
        state[...] = gc * s_prev + _dot_tn(kw, vc)
        return o

    def fwd(sl):
        return chunk(sl, sf, decay_f, wq_f, ws_f, gc_f)

    def bwd(sl):
        return chunk(sl, sb, decay_b, wq_b, ws_b, gc_b)

    def finish(sl, o):
        mu = jnp.mean(o, axis=-1, keepdims=True)
        dev = o - mu
        var = jnp.mean(dev * dev, axis=-1, keepdims=True)
        on = dev * lax.rsqrt(var + RMS_EPS) * gn_ref[...]
        o_ref[sl, :] = (g_ref[sl, :].astype(F32) * on).astype(BF16)

    half = nc // 2

    def first_half(i, carry):
        sl_f, sl_b = rows(i), rows(nc - 1 - i)
        obuf[sl_f, :] = fwd(sl_f)
        obuf[sl_b, :] = bwd(sl_b)
        return carry

    lax.fori_loop(0, half, first_half, 0)
    if nc % 2:
        sl = rows(half)
        finish(sl, fwd(sl) + bwd(sl))

    def second_half(i, carry):
        sl_f, sl_b = rows(i), rows(nc - 1 - i)
        finish(sl_f, fwd(sl_f) + obuf[sl_f, :])
        finish(sl_b, bwd(sl_b) + obuf[sl_b, :])
        return carry

    lax.fori_loop(half + nc % 2, nc, second_half, 0)
    if prompt:
        sfin_ref[0] = sf[...]
        sfin_ref[1] = sb[...]


def _softplus(x):
    return jnp.maximum(x, 0.0) + jnp.log1p(jnp.exp(-jnp.abs(x)))


def _sigmoid(x):
    return 0.5 * jnp.tanh(0.5 * x) + 0.5


def _lru_kernel(xr_ref, prev_ref, next_ref, act_ref, cw_ref, cb_ref, wg_ref, br_ref, bi_ref, lam_ref,
                h0_ref, y_ref, hfin_ref, hf_buf, xc_buf, a_s, u_s, hb_s, hc, *, nc):
    tc, d_rnn = xr_ref.shape
    bw = d_rnn // LRU_BLOCKS
    s = pl.program_id(1)
    is_b = s >= nc
    c = jnp.where(is_b, 2 * nc - 1 - s, s)
    base = pl.multiple_of(c * tc, tc)

    @pl.when(jnp.logical_not(is_b))
    def _():
        x = xr_ref[...]
        rowi = lax.broadcasted_iota(jnp.int32, (tc, d_rnn), 0)
        p1 = jnp.where(c > 0, prev_ref[V7X_SUBLANES - 1:V7X_SUBLANES, :], 0.0)
        p2 = jnp.where(c > 0, prev_ref[V7X_SUBLANES - 2:V7X_SUBLANES - 1, :], 0.0)
        n1 = jnp.where(c < nc - 1, next_ref[0:1, :], 0.0)
        xm1 = jnp.where(rowi == 0, p1, pltpu.roll(x, 1, 0))
        xm2 = jnp.where(rowi == 0, p2, jnp.where(rowi == 1, p1, pltpu.roll(x, 2, 0)))
        xp1 = jnp.where(rowi == tc - 1, n1, pltpu.roll(x, tc - 1, 0))
        xc_buf[pl.ds(base, tc), :] = (cw_ref[0:1, :] * xm2 + cw_ref[1:2, :] * xm1 + cw_ref[2:3, :] * x
                                      + cw_ref[3:4, :] * xp1 + cb_ref[...])

    xc = xc_buf[pl.ds(base, tc), :]
    xcb = xc.astype(BF16)
    sp = _softplus(-lam_ref[...])
    for n in range(LRU_BLOCKS):
        cs = slice(bw * n, bw * (n + 1))
        z = _dot(xcb[:, cs], wg_ref[n])
        r = _sigmoid(z[:, :bw] + br_ref[:, cs])
        i = _sigmoid(z[:, bw:] + bi_ref[:, cs])
        log_a = -LRU_C * r * sp[:, cs]
        a = jnp.exp(log_a)
        u = jnp.sqrt(-jnp.tanh(log_a) * (a * a + 1.0)) * (i * xc[:, cs])
        a_s[:, cs] = a
        u_s[:, cs] = u

    @pl.when(s == 0)
    def _():
        hc[...] = h0_ref[0:1, :]

    @pl.when(s == nc)
    def _():
        hc[...] = h0_ref[1:2, :]

    @pl.when(jnp.logical_not(is_b))
    def _():
        def group(i, h):
            t0 = pl.multiple_of(i * V7X_SUBLANES, V7X_SUBLANES)
            for r in range(V7X_SUBLANES):
                h = a_s[pl.ds(t0 + r, 1), :] * h + u_s[pl.ds(t0 + r, 1), :]
                hf_buf[pl.ds(base + t0 + r, 1), :] = h
            return h
        h = lax.fori_loop(0, tc // V7X_SUBLANES, group, hc[...])
        hc[...] = h

        @pl.when(s == nc - 1)
        def _():
            hfin_ref[0:1, :] = h

    @pl.when(is_b)
    def _():
        def group(i, h):
            t0 = pl.multiple_of(tc - V7X_SUBLANES * (i + 1), V7X_SUBLANES)
            for r in reversed(range(V7X_SUBLANES)):
                h = a_s[pl.ds(t0 + r, 1), :] * h + u_s[pl.ds(t0 + r, 1), :]
                hb_s[pl.ds(t0 + r, 1), :] = h
            return h
        h = lax.fori_loop(0, tc // V7X_SUBLANES, group, hc[...])
        hc[...] = h
        y_ref[...] = (act_ref[...] * (hf_buf[pl.ds(base, tc), :] + hb_s[...])).astype(BF16)

        @pl.when(s == 2 * nc - 1)
        def _():
            hfin_ref[1:2, :] = h


def _out_mlp_kernel(*refs, n_prompt, split_x, split_out):
    x, rest = _read_x(refs, n_prompt, split_x)
    yp_ref, ys_ref, mod_ref, gm_ref, wo_ref, wu_ref, wd_ref = rest[:7]
    d = x.shape[1]
    d_ff = wu_ref.shape[1]
    is_prompt = pl.program_id(0) < n_prompt
    y = jnp.where(is_prompt, yp_ref[...], ys_ref[...])
    x1 = x + mod_ref[:, 2 * d:3 * d] * _dot(y, wo_ref[...])
    hn = _modulate(x1, gm_ref[...], mod_ref[:, 3 * d:4 * d], mod_ref[:, 4 * d:5 * d]).astype(BF16)
    acc = jnp.zeros(x1.shape, F32)
    for c in range(d_ff // FF_CHUNK):
        hh = jnp.maximum(_dot(hn, wu_ref[:, FF_CHUNK * c:FF_CHUNK * (c + 1)]), 0.0)
        acc = acc + _dot((hh * hh).astype(BF16), wd_ref[FF_CHUNK * c:FF_CHUNK * (c + 1), :])
    out = x1 + mod_ref[:, 5 * d:6 * d] * acc
    if split_out:
        op_ref, os_ref = rest[7:]

        @pl.when(is_prompt)
        def _():
            op_ref[...] = out

        @pl.when(jnp.logical_not(is_prompt))
        def _():
            os_ref[...] = out
    else:
        rest[7][...] = out


class _Dims:
    def __init__(self, bp, lp, bs, ts, d):
        assert bp * lp == ts, "prompt tokens must fill exactly one segment"
        self.bp, self.lp, self.bs, self.ts, self.d = bp, lp, bs, ts, d
        self.nseg = 1 + bs
        self.n = self.nseg * ts
        self.tm = min(TOKEN_TILE, ts)
        assert ts % self.tm == 0 and ts % ATT_BLOCK == 0
        assert ts % RET_KERNEL_CHUNK == 0 and lp % min(RET_KERNEL_CHUNK, lp) == 0
        self.tps = ts // self.tm
        self.ntiles = self.n // self.tm


def _in_kernel(body, dm, split_x):
    return functools.partial(body, n_prompt=dm.tps, split_x=split_x)


def _attn_layer(dm, layer, slot, xs, norm_mix, mods, w_in, q_gain, k_gain, sink, cache_k, cache_v, rope):
    d, tm, tps, n = dm.d, dm.tm, dm.tps, dm.n
    split_x = len(xs) == 2
    hd = ATT_HEAD_DIM
    q_w, kv_w = ATT_HEADS * hd, ATT_KV_HEADS * hd
    gain = jnp.concatenate([jnp.tile(q_gain, ATT_HEADS), jnp.tile(k_gain, ATT_KV_HEADS)])[None, :]
    gw = q_w + kv_w
    assert (q_w + kv_w) % gw == 0 and gw % V7X_LANES == 0 and gw // hd <= V7X_LANES
    head_of_lane = np.arange(gw) // hd
    hsum = jnp.asarray(head_of_lane[:, None] == np.arange(V7X_LANES)[None, :], BF16)
    hexp = jnp.asarray(np.arange(V7X_LANES)[:, None] == head_of_lane[None, :], BF16)
    cos, sin = rope
    nt = dm.ntiles
    cur = lambda i: jnp.minimum(i, nt - 1)
    prev = lambda i: jnp.maximum(i - 1, 0)
    tok = lambda w: pl.BlockSpec((tm, w), lambda i: (prev(i), 0))
    q, k, v, kf, vf = pl.pallas_call(
        _in_kernel(_attn_in_kernel, dm, split_x),
        grid=(nt + 1,),
        in_specs=_token_specs(tm, tps, d, layer, split_x, cur) + [
            _resident(w_in.shape), _resident(gain.shape)] + _rope_specs(tm, tps, V7X_LANES, prev) + [
            _resident(hsum.shape), _resident(hexp.shape)],
        out_specs=[tok(q_w), tok(2 * kv_w), tok(2 * kv_w), tok(kv_w), tok(kv_w)],
        out_shape=[jax.ShapeDtypeStruct((n, q_w), BF16), jax.ShapeDtypeStruct((n, 2 * kv_w), BF16),
                   jax.ShapeDtypeStruct((n, 2 * kv_w), BF16), jax.ShapeDtypeStruct((n, kv_w), F32),
                   jax.ShapeDtypeStruct((n, kv_w), F32)],
        scratch_shapes=[pltpu.VMEM((2, tm, q_w + 2 * kv_w), F32)],
        compiler_params=_cparams(("arbitrary",)),
        name="attn_in",
    )(*xs, norm_mix, mods, w_in, gain, cos, sin, hsum, hexp)
    kf, vf = kf[:dm.ts], vf[:dm.ts]

    smem = pl.BlockSpec(memory_space=pltpu.SMEM)
    lp, bp, bs, ts = dm.lp, dm.bp, dm.bs, dm.ts
    y_p = pl.pallas_call(
        _attn_prompt_kernel,
        grid=(bp,),
        in_specs=[smem, pl.BlockSpec((lp, q_w), lambda b: (b, 0)),
                  pl.BlockSpec((lp, 2 * kv_w), lambda b: (b, 0)),
                  pl.BlockSpec((lp, 2 * kv_w), lambda b: (b, 0))],
        out_specs=pl.BlockSpec((lp, q_w), lambda b: (b, 0)),
        out_shape=jax.ShapeDtypeStruct((ts, q_w), BF16),
        compiler_params=_cparams(("arbitrary",)),
        name="attn_prompt",
    )(sink, q, k, v)

    blk = ATT_BLOCK
    nb = ts // blk
    nblocks = bs * nb
    past = cache_k.shape[2]
    nk = 3 * blk + past
    c1 = lambda i: jnp.minimum(i, nblocks - 1)
    c3 = lambda i: jnp.clip(i - 2, 0, nblocks - 1)
    cur = lambda c: lambda i: (nb + c(i), 0)
    prv = lambda c: lambda i: (nb + c(i) - jnp.where(c(i) % nb > 0, 1, 0), 0)
    nxt = lambda c: lambda i: (nb + c(i) + jnp.where(c(i) % nb < nb - 1, 1, 0), 0)
    kv_spec = lambda f: pl.BlockSpec((blk, 2 * kv_w), f)
    ctx_spec = lambda c: pl.BlockSpec((None, None, past, kv_w), lambda i: (c(i) // nb, slot, 0, 0))
    rows = ATT_GROUPS * blk
    y_s = pl.pallas_call(
        functools.partial(_attn_sample_kernel, nb=nb, nblocks=nblocks),
        grid=(nblocks + 2,),
        in_specs=[smem, pl.BlockSpec((blk, q_w), cur(c1)),
                  kv_spec(prv(c1)), kv_spec(cur(c1)), kv_spec(nxt(c1)),
                  kv_spec(prv(c3)), kv_spec(cur(c3)), kv_spec(nxt(c3)), ctx_spec(c1), ctx_spec(c3)],
        out_specs=pl.BlockSpec((blk, q_w), lambda i: (c3(i), 0)),
        out_shape=jax.ShapeDtypeStruct((bs * ts, q_w), BF16),
        scratch_shapes=[pltpu.VMEM((nk, 2 * kv_w), BF16), pltpu.VMEM((nk, 2 * kv_w), BF16),
                        pltpu.VMEM((2, ATT_KV_HEADS, rows, nk), F32),
                        pltpu.VMEM((2, ATT_KV_HEADS, rows, nk), BF16),
                        pltpu.VMEM((2, ATT_KV_HEADS, rows, V7X_LANES), F32)],
        compiler_params=_cparams(("arbitrary",)),
        name="attn_sample",
    )(sink, q, k, k, k, v, v, v, cache_k, cache_v)
    return y_p, y_s, kf, vf


def _ret_layer(dm, layer, slot, xs, norm_mix, mods, w_in, gn_gain, log_decay, state, rope):
    d, tm, tps, n = dm.d, dm.tm, dm.tps, dm.n
    split_x = len(xs) == 2
    e1, ev = RET_HEADS * RET_DK, RET_HEADS * RET_DV
    cos, sin = rope
    tok = lambda w: pl.BlockSpec((tm, w), lambda i: (i, 0))
    q, k, v, g = pl.pallas_call(
        _in_kernel(_ret_in_kernel, dm, split_x),
        grid=(dm.ntiles,),
        in_specs=(_token_specs(tm, tps, d, layer, split_x) + [_resident(w_in.shape)]
                  + _rope_specs(tm, tps, RET_DK)),
        out_specs=[tok(e1), tok(e1), tok(ev), tok(ev)],
        out_shape=[jax.ShapeDtypeStruct((n, e1), BF16), jax.ShapeDtypeStruct((n, e1), BF16),
                   jax.ShapeDtypeStruct((n, ev), BF16), jax.ShapeDtypeStruct((n, ev), BF16)],
        compiler_params=_cparams(("arbitrary",)),
        name="ret_in",
    )(*xs, norm_mix, mods, w_in, cos, sin)

    smem = pl.BlockSpec(memory_space=pltpu.SMEM)
    gn = gn_gain[None, :]
    lp, bp, bs, ts = dm.lp, dm.bp, dm.bs, dm.ts
    state_spec = lambda f: pl.BlockSpec((None, None, 2, None, RET_DK, RET_DV), f)

    def call(t_len, nbatch, seg0, prompt, name):
        rows = lambda w: pl.BlockSpec((t_len, w), lambda b, h: (seg0 + b, h))
        in_specs = [smem, rows(RET_DK), rows(RET_DK), rows(RET_DV), rows(RET_DV)]
        args = [log_decay, q, k, v, g]
        out_shape = [jax.ShapeDtypeStruct((nbatch * t_len, ev), BF16)]
        out_specs = [pl.BlockSpec((t_len, RET_DV), lambda b, h: (b, h))]
        if prompt:
            out_shape.append(jax.ShapeDtypeStruct((nbatch, 1, 2, RET_HEADS, RET_DK, RET_DV), F32))
            out_specs.append(state_spec(lambda b, h: (b, 0, 0, h, 0, 0)))
        else:
            in_specs.append(state_spec(lambda b, h: (b, slot, 0, h, 0, 0)))
            args.append(state)
        in_specs.append(pl.BlockSpec((1, RET_DV), lambda b, h: (0, h)))
        args.append(gn)
        return pl.pallas_call(
            functools.partial(_ret_kernel, t_len=t_len, c_len=min(RET_KERNEL_CHUNK, t_len), prompt=prompt),
            grid=(nbatch, RET_HEADS),
            in_specs=in_specs, out_specs=out_specs, out_shape=out_shape,
            scratch_shapes=[pltpu.VMEM((t_len, RET_DV), F32), pltpu.VMEM((RET_DK, RET_DV), F32),
                            pltpu.VMEM((RET_DK, RET_DV), F32)],
            compiler_params=_cparams(("arbitrary", "arbitrary")),
            name=name,
        )(*args)

    y_p, s_new = call(lp, bp, 0, True, "ret_prompt")
    (y_s,) = call(ts, bs, 1, False, "ret_sample")
    return y_p, y_s, s_new


def _lru_layer(dm, layer, slot, xs, norm_mix, mods, w_in, conv_w, conv_b, w_r, b_r, w_i, b_i, lam, state):
    d, tm, tps, n = dm.d, dm.tm, dm.tps, dm.n
    split_x = len(xs) == 2
    d_rnn = w_in.shape[1] // 2
    tok = lambda w: pl.BlockSpec((tm, w), lambda i: (i, 0))
    act, xr = pl.pallas_call(
        _in_kernel(_lru_in_kernel, dm, split_x),
        grid=(dm.ntiles,),
        in_specs=_token_specs(tm, tps, d, layer, split_x) + [_resident(w_in.shape)],
        out_specs=[tok(d_rnn), tok(d_rnn)],
        out_shape=[jax.ShapeDtypeStruct((n, d_rnn), F32), jax.ShapeDtypeStruct((n, d_rnn), F32)],
        compiler_params=_cparams(("arbitrary",)),
        name="lru_in",
    )(*xs, norm_mix, mods, w_in)

    wg = jnp.concatenate([w_r, w_i], axis=-1).astype(BF16)
    vec = lambda a: a[:, None, :]
    lp, bp, bs, ts = dm.lp, dm.bp, dm.bs, dm.ts
    sub = V7X_SUBLANES

    def call(t_len, nbatch, row0, h0, h0_slot, name):
        tc = t_len if t_len <= 256 else min(512, t_len // 2)
        nc = t_len // tc
        cb = row0 // tc
        early = lambda s: jnp.minimum(s, nc - 1)
        late = lambda s: jnp.where(s >= nc, 2 * nc - 1 - s, nc - 1)
        dirv = lambda s: jnp.where(s >= nc, 1, 0)
        r8 = tc // sub
        first8 = lambda b: (row0 + b * t_len) // sub
        prev8 = lambda b, s: (jnp.maximum(first8(b) + early(s) * r8 - 1, first8(b)), 0)
        next8 = lambda b, s: (jnp.minimum(first8(b) + (early(s) + 1) * r8, first8(b) + t_len // sub - 1), 0)
        dspec = lambda shape: pl.BlockSpec((None,) + shape, lambda b, s: (dirv(s),) + (0,) * len(shape))
        return pl.pallas_call(
            functools.partial(_lru_kernel, nc=nc),
            grid=(nbatch, 2 * nc),
            in_specs=[pl.BlockSpec((tc, d_rnn), lambda b, s: (cb + b * nc + early(s), 0)),
                      pl.BlockSpec((sub, d_rnn), prev8), pl.BlockSpec((sub, d_rnn), next8),
                      pl.BlockSpec((tc, d_rnn), lambda b, s: (cb + b * nc + late(s), 0)),
                      pl.BlockSpec((CONV_W, d_rnn), lambda b, s: (0, 0)),
                      pl.BlockSpec((1, d_rnn), lambda b, s: (0, 0)),
                      dspec(wg.shape[1:]), dspec((1, d_rnn)), dspec((1, d_rnn)), dspec((1, d_rnn)),
                      pl.BlockSpec((None, None, 2, d_rnn), lambda b, s: (b, h0_slot, 0, 0))],
            out_specs=[pl.BlockSpec((tc, d_rnn), lambda b, s: (b * nc + late(s), 0)),
                       pl.BlockSpec((None, 2, d_rnn), lambda b, s: (b, 0, 0))],
            out_shape=[jax.ShapeDtypeStruct((nbatch * t_len, d_rnn), BF16),
                       jax.ShapeDtypeStruct((nbatch, 2, d_rnn), F32)],
            scratch_shapes=[pltpu.VMEM((t_len, d_rnn), F32), pltpu.VMEM((t_len, d_rnn), F32),
                            pltpu.VMEM((tc, d_rnn), F32), pltpu.VMEM((tc, d_rnn), F32),
                            pltpu.VMEM((tc, d_rnn), F32), pltpu.VMEM((1, d_rnn), F32)],
            compiler_params=_cparams(("arbitrary", "arbitrary")),
            name=name,
        )(xr, xr, xr, act, conv_w, conv_b[None, :], wg, vec(b_r), vec(b_i), vec(lam), h0)

    y_p, h_new = call(lp, bp, 0, jnp.zeros((bp, 1, 2, d_rnn), F32), 0, "lru_prompt")
    y_s, _ = call(ts, bs, ts, state, slot, "lru_sample")
    return y_p, y_s, h_new


def _out_mlp(dm, layer, xs, y_p, y_s, norm_mlp, mods, w_out, w_up, w_down, split_out):
    d, tm, tps, n = dm.d, dm.tm, dm.tps, dm.n
    split_x = len(xs) == 2
    din = w_out.shape[0]
    first = lambda i: (jnp.minimum(i, tps - 1), 0)
    second = lambda i: (jnp.maximum(i - tps, 0), 0)
    if split_out:
        out_specs = [pl.BlockSpec((tm, d), first), pl.BlockSpec((tm, d), second)]
        out_shape = [jax.ShapeDtypeStruct((dm.ts, d), F32), jax.ShapeDtypeStruct((n - dm.ts, d), F32)]
    else:
        out_specs = pl.BlockSpec((tm, d), lambda i: (i, 0))
        out_shape = jax.ShapeDtypeStruct((n, d), F32)
    return pl.pallas_call(
        functools.partial(_out_mlp_kernel, n_prompt=tps, split_x=split_x, split_out=split_out),
        grid=(dm.ntiles,),
        in_specs=_x_specs(tm, tps, d, split_x) + [
            pl.BlockSpec((tm, din), first), pl.BlockSpec((tm, din), second),
            pl.BlockSpec((None, None, 1, N_MOD * d), lambda i: (layer, i // tps, 0, 0)),
            pl.BlockSpec((None, 1, d), lambda i: (layer, 0, 0)),
            _resident(w_out.shape), _layer_resident(w_up.shape, layer), _layer_resident(w_down.shape, layer)],
        out_specs=out_specs, out_shape=out_shape,
        compiler_params=_cparams(("arbitrary",)),
        name="out_mlp",
    )(*xs, y_p, y_s, mods, norm_mlp, w_out, w_up, w_down)


def kernel(x_prompt, x_sample, cache_attn_k, cache_attn_v, state_ret, state_lru, c, c_ctx, norm_mix, norm_mlp, w_ada, b_ada, w_up, w_down, attn_w_in, attn_w_out, attn_q_gain, attn_k_gain, attn_sink, ret_w_in, ret_w_out, ret_gn_gain, ret_log_decay, lru_w_in, lru_conv_w, lru_conv_b, lru_w_r, lru_b_r, lru_w_i, lru_b_i, lru_lambda, lru_w_out):
    bp, lp, d = x_prompt.shape
    bs, ts, _ = x_sample.shape
    dm = _Dims(bp, lp, bs, ts, d)
    depth = w_ada.shape[0]
    assert dm.nseg <= MOD_ROWS

    xs = (x_prompt.reshape(ts, d), x_sample.reshape(bs * ts, d))
    cvec = jnp.concatenate([c_ctx[None, :], c, jnp.zeros((MOD_ROWS - dm.nseg, d), F32)], axis=0)
    mods = _modulation(cvec, w_ada, b_ada)[:, :dm.nseg].reshape(depth, dm.nseg, 1, N_MOD * d)
    nmix = norm_mix[:, None, :]
    nmlp = norm_mlp[:, None, :]

    rope_attn = tuple(jnp.tile(t, (1, 1, 2)) for t in _rope_tables(ts, ATT_HEAD_DIM // 2))
    rope_ret = _rope_tables(ts, RET_DK // 2)
    kv_w = ATT_KV_HEADS * ATT_HEAD_DIM
    past = cache_attn_k.shape[2]
    ck = cache_attn_k.reshape(bs, -1, past, kv_w)
    cv = cache_attn_v.reshape(bs, -1, past, kv_w)

    w_up_b, w_down_b = w_up.astype(BF16), w_down.astype(BF16)

    new_k, new_v, new_ret, new_lru = [], [], [], []
    for layer in range(depth):
        kind, slot = layer % 3, layer // 3
        if kind == 0:
            y_p, y_s, kf, vf = _attn_layer(dm, layer, slot, xs, nmix, mods, attn_w_in[slot].astype(BF16),
                                           attn_q_gain[slot], attn_k_gain[slot], attn_sink[slot],
                                           ck, cv, rope_attn)
            new_k.append(kf.reshape(bp, lp, ATT_KV_HEADS, ATT_HEAD_DIM))
            new_v.append(vf.reshape(bp, lp, ATT_KV_HEADS, ATT_HEAD_DIM))
            w_out = attn_w_out[slot]
        elif kind == 1:
            y_p, y_s, s_new = _ret_layer(dm, layer, slot, xs, nmix, mods, ret_w_in[slot].astype(BF16),
                                         ret_gn_gain[slot], ret_log_decay[slot], state_ret, rope_ret)
            new_ret.append(s_new)
            w_out = ret_w_out[slot]
        else:
            y_p, y_s, h_new = _lru_layer(dm, layer, slot, xs, nmix, mods, lru_w_in[slot].astype(BF16),
                                         lru_conv_w[slot], lru_conv_b[slot], lru_w_r[slot], lru_b_r[slot],
                                         lru_w_i[slot], lru_b_i[slot], lru_lambda[slot], state_lru)
            new_lru.append(h_new)
            w_out = lru_w_out[slot]
        last = layer == depth - 1
        out = _out_mlp(dm, layer, xs, y_p, y_s, nmlp, mods, w_out.astype(BF16), w_up_b, w_down_b,
                       split_out=last)
        xs = tuple(out) if last else (out,)

    y_prompt = xs[0].reshape(bp, lp, d)
    y_sample = xs[1].reshape(bs, ts, d)
    return (y_prompt, y_sample, jnp.stack(new_k, axis=1), jnp.stack(new_v, axis=1),
            jnp.concatenate(new_ret, axis=1), jnp.stack(new_lru, axis=1))
```

```python
import functools

import jax
import jax.numpy as jnp
import numpy as np
from jax import lax
from jax.experimental import pallas as pl
from jax.experimental.pallas import tpu as pltpu

F32 = jnp.float32
BF16 = jnp.bfloat16

N_MOD = 6
RMS_EPS = 1e-6
ROPE_BASE = 10000.0
NEG_INF = -1e30
LOG2E = 1.4426950408889634
GRID_W = 64
ATT_HEADS = 16
ATT_KV_HEADS = 4
ATT_HEAD_DIM = 64
ATT_GROUPS = ATT_HEADS // ATT_KV_HEADS
ATT_BLOCK = 128
RET_HEADS = 4
RET_DK = 256
RET_DV = 512
LRU_BLOCKS = 8
CONV_W = 4
LRU_C = 8.0

V7X_LANES = 128
V7X_SUBLANES = 8
V7X_VMEM_BYTES = 64 * 1024 * 1024
VMEM_LIMIT = V7X_VMEM_BYTES - 8 * 1024 * 1024

TOKEN_TILE = 512
MOD_ROWS = 16
FF_CHUNK = 1024
RET_KERNEL_CHUNK = 256


def _cparams(sem):
    return pltpu.CompilerParams(dimension_semantics=sem, vmem_limit_bytes=VMEM_LIMIT)


def _resident(shape):
    zeros = (0,) * len(shape)
    return pl.BlockSpec(shape, lambda *_: zeros, pipeline_mode=pl.Buffered(1))


def _layer_resident(shape, layer):
    zeros = (0,) * (len(shape) - 1)
    return pl.BlockSpec((None,) + tuple(shape[1:]), lambda *_: (layer,) + zeros, pipeline_mode=pl.Buffered(1))


def _dot(a, b):
    return jnp.dot(a, b, preferred_element_type=F32)


def _dot_nt(a, b):
    return lax.dot_general(a, b, (((1,), (1,)), ((), ())), preferred_element_type=F32)


def _dot_tn(a, b):
    return lax.dot_general(a, b, (((0,), (0,)), ((), ())), preferred_element_type=F32)


def _silu(x):
    return x * jax.nn.sigmoid(x)


def _modulate(x, g, shift, scale):
    ms = jnp.mean(x * x, axis=-1, keepdims=True)
    y = x * lax.rsqrt(ms + RMS_EPS) * g
    return y * (1.0 + scale) + shift


def _split_bf16(x):
    hi = x.astype(BF16)
    lo = (x - hi.astype(F32)).astype(BF16)
    return hi, lo


def _read_x(refs, n_prompt, split_x):
    if split_x:
        x = jnp.where(pl.program_id(0) < n_prompt, refs[0][...], refs[1][...])
        return x, refs[2:]
    return refs[0][...], refs[1:]


def _mod_kernel(c_ref, w_ref, b_ref, o_ref):
    s = _silu(c_ref[...]).astype(BF16)
    o_ref[...] = _dot(s, w_ref[...].astype(BF16)) + b_ref[...]


def _modulation(cvec, w_ada, b_ada):
    depth, d, w = w_ada.shape
    tn = 1536
    return pl.pallas_call(
        _mod_kernel,
        grid=(depth, w // tn),
        in_specs=[pl.BlockSpec((MOD_ROWS, d), lambda l, j: (0, 0)),
                  pl.BlockSpec((None, d, tn), lambda l, j: (l, 0, j)),
                  pl.BlockSpec((None, 1, tn), lambda l, j: (l, 0, j))],
        out_specs=pl.BlockSpec((None, MOD_ROWS, tn), lambda l, j: (l, 0, j)),
        out_shape=jax.ShapeDtypeStruct((depth, MOD_ROWS, w), F32),
        compiler_params=_cparams(("arbitrary", "arbitrary")),
        name="modulation",
    )(cvec, w_ada, b_ada.reshape(depth, 1, w))


def _rope_tables(t_len, sect):
    half = sect // 2
    inv = ROPE_BASE ** (-jnp.arange(half, dtype=F32) / half)
    t = jnp.arange(t_len, dtype=jnp.int32)

    def one(pos):
        ang = pos.astype(F32)[:, None] * inv[None, :]
        c, s = jnp.cos(ang), jnp.sin(ang)
        return jnp.concatenate([c, c], axis=1), jnp.concatenate([-s, s], axis=1)

    cr, sr = one(t // GRID_W)
    cc, sc = one(t % GRID_W)
    cos = jnp.concatenate([cr, cc], axis=1)
    sin = jnp.concatenate([sr, sc], axis=1)
    return (jnp.stack([jnp.ones_like(cos), cos]), jnp.stack([jnp.zeros_like(sin), sin]))


def _dup_halves(x, lane_low):
    r = pltpu.roll(x, 64, 1)
    return jnp.where(lane_low, x, r), jnp.where(lane_low, r, x)


def _attn_in_kernel(*refs, n_prompt, split_x):
    x, rest = _read_x(refs, n_prompt, split_x)
    (g_ref, mod_ref, w_ref, gain_ref, cos_ref, sin_ref, hsum_ref, hexp_ref,
     q_out, k_out, v_out, kf_out, vf_out, proj_s) = rest
    tm, d = x.shape
    i = pl.program_id(0)
    qk_w = hsum_ref.shape[0]
    n_q = ATT_HEADS * ATT_HEAD_DIM // V7X_LANES

    @pl.when(i == 0)
    def _():
        proj_s[1] = jnp.zeros(proj_s.shape[1:], F32)

    def epilogue(proj):
        lane = lax.broadcasted_iota(jnp.int32, (tm, V7X_LANES), 1)
        first = (lane % 32) < 16
        low = lane < 64
        cos = cos_ref[...]
        sin = sin_ref[...]
        qk = proj[:, :qk_w]
        ssum = _dot((qk * qk).astype(BF16), hsum_ref[...])
        r_hi, r_lo = _split_bf16(lax.rsqrt(ssum * (1.0 / ATT_HEAD_DIM) + RMS_EPS))
        qkn = qk * (_dot(r_hi, hexp_ref[...]) + _dot(r_lo, hexp_ref[...])) * gain_ref[...]
        for j in range(qk_w // V7X_LANES):
            xj = qkn[:, V7X_LANES * j:V7X_LANES * (j + 1)]
            rot = jnp.where(first, pltpu.roll(xj, V7X_LANES - 16, 1), pltpu.roll(xj, 16, 1))
            yj = xj * cos + rot * sin
            if j < n_q:
                q_out[:, V7X_LANES * j:V7X_LANES * (j + 1)] = (yj * (ATT_HEAD_DIM ** -0.5 * LOG2E)).astype(BF16)
            else:
                jj = j - n_q
                kf_out[:, V7X_LANES * jj:V7X_LANES * (jj + 1)] = yj
                a, b = _dup_halves(yj, low)
                k_out[:, V7X_LANES * (2 * jj):V7X_LANES * (2 * jj + 1)] = a.astype(BF16)
                k_out[:, V7X_LANES * (2 * jj + 1):V7X_LANES * (2 * jj + 2)] = b.astype(BF16)
        v = proj[:, qk_w:]
        vf_out[...] = v
        for jj in range(v.shape[1] // V7X_LANES):
            a, b = _dup_halves(v[:, V7X_LANES * jj:V7X_LANES * (jj + 1)], low)
            v_out[:, V7X_LANES * (2 * jj):V7X_LANES * (2 * jj + 1)] = a.astype(BF16)
            v_out[:, V7X_LANES * (2 * jj + 1):V7X_LANES * (2 * jj + 2)] = b.astype(BF16)

    def step(slot):
        epilogue(proj_s[1 - slot])
        h = _modulate(x, g_ref[...], mod_ref[:, 0:d], mod_ref[:, d:2 * d]).astype(BF16)
        proj_s[slot] = _dot(h, w_ref[...])

    @pl.when(i % 2 == 0)
    def _():
        step(0)

    @pl.when(i % 2 == 1)
    def _():
        step(1)


def _ret_in_kernel(*refs, n_prompt, split_x):
    x, (g_ref, mod_ref, w_ref, cos_ref, sin_ref, q_out, k_out, v_out, g_out) = _read_x(refs, n_prompt, split_x)
    d = x.shape[1]
    e1 = RET_HEADS * RET_DK
    e3 = 2 * e1 + RET_HEADS * RET_DV
    h = _modulate(x, g_ref[...], mod_ref[:, 0:d], mod_ref[:, d:2 * d]).astype(BF16)
    n_chunks = e1 // V7X_LANES
    for sec, out in ((0, q_out), (1, k_out)):
        p = _dot(h, w_ref[:, sec * e1:(sec + 1) * e1])
        for j in range(n_chunks):
            s = j % 2
            xj = p[:, V7X_LANES * j:V7X_LANES * (j + 1)]
            yj = (xj * cos_ref[:, V7X_LANES * s:V7X_LANES * (s + 1)]
                  + pltpu.roll(xj, 64, 1) * sin_ref[:, V7X_LANES * s:V7X_LANES * (s + 1)])
            if sec == 1:
                yj = yj * (RET_DK ** -0.5)
            out[:, V7X_LANES * j:V7X_LANES * (j + 1)] = yj.astype(BF16)
    v_out[...] = _dot(h, w_ref[:, 2 * e1:e3]).astype(BF16)
    g_out[...] = _silu(_dot(h, w_ref[:, e3:])).astype(BF16)


def _lru_in_kernel(*refs, n_prompt, split_x):
    x, (g_ref, mod_ref, w_ref, act_out, xr_out) = _read_x(refs, n_prompt, split_x)
    d = x.shape[1]
    d_rnn = act_out.shape[1]
    h = _modulate(x, g_ref[...], mod_ref[:, 0:d], mod_ref[:, d:2 * d]).astype(BF16)
    act_out[...] = jax.nn.gelu(_dot(h, w_ref[:, :d_rnn]), approximate=True)
    xr_out[...] = _dot(h, w_ref[:, d_rnn:])


def _x_specs(tm, tps, d, split_x, tile=lambda i: i):
    if split_x:
        return [pl.BlockSpec((tm, d), lambda i: (jnp.minimum(tile(i), tps - 1), 0)),
                pl.BlockSpec((tm, d), lambda i: (jnp.maximum(tile(i) - tps, 0), 0))]
    return [pl.BlockSpec((tm, d), lambda i: (tile(i), 0))]


def _token_specs(tm, tps, d, layer, split_x, tile=lambda i: i):
    return _x_specs(tm, tps, d, split_x, tile) + [
        pl.BlockSpec((None, 1, d), lambda i: (layer, 0, 0)),
        pl.BlockSpec((None, None, 1, N_MOD * d), lambda i: (layer, tile(i) // tps, 0, 0))]


def _rope_specs(tm, tps, width, tile=lambda i: i):
    spec = pl.BlockSpec((None, tm, width), lambda i: (jnp.minimum(tile(i) // tps, 1), tile(i) % tps, 0))
    return [spec, spec]


def _attend(q_ref, k_ref, v_ref, bias, sink_ref, o_ref):
    nq = q_ref.shape[0]
    w = V7X_LANES
    low = lax.broadcasted_iota(jnp.int32, (nq, w), 1) < 64
    for h in range(ATT_KV_HEADS):
        hcol = slice(w * h, w * (h + 1))
        qs = []
        for g in range(ATT_GROUPS):
            a = ATT_GROUPS * h + g
            piece = q_ref[:, w * (a // 2):w * (a // 2 + 1)]
            qs.append(jnp.where(low if a % 2 == 0 else jnp.logical_not(low), piece, jnp.zeros_like(piece)))
        s = _dot_nt(jnp.concatenate(qs, axis=0), k_ref[:, hcol])
        ps, dens = [], []
        for g in range(ATT_GROUPS):
            sg = s[nq * g:nq * (g + 1)]
            if bias is not None:
                sg = sg + bias
            snk = sink_ref[ATT_GROUPS * h + g] * LOG2E
            m = jnp.maximum(jnp.max(sg, axis=-1, keepdims=True), snk)
            p = jnp.exp2(sg - m)
            dens.append(jnp.sum(p, axis=-1, keepdims=True) + jnp.exp2(snk - m))
            ps.append(p.astype(BF16))
        o = _dot(jnp.concatenate(ps, axis=0), v_ref[:, hcol])
        for pair in range(ATT_GROUPS // 2):
            g0, g1 = 2 * pair, 2 * pair + 1
            o0 = o[nq * g0:nq * (g0 + 1)] / dens[g0]
            o1 = o[nq * g1:nq * (g1 + 1)] / dens[g1]
            col = (ATT_GROUPS * h) // 2 + pair
            o_ref[:, w * col:w * (col + 1)] = jnp.where(low, o0, o1).astype(BF16)


def _attn_prompt_kernel(sink_ref, q_ref, k_ref, v_ref, o_ref):
    _attend(q_ref, k_ref, v_ref, None, sink_ref, o_ref)


def _attn_sample_kernel(sink_ref, q_ref, kp_ref, kc_ref, kn_ref, vp_ref, vc_ref, vn_ref,
                        kctx_ref, vctx_ref, o_ref, kall, vall, s_buf, p_buf, den_buf, *, nb, nblocks):
    i = pl.program_id(0)
    blk, w = ATT_BLOCK, V7X_LANES
    nq = blk
    n_ctx = kctx_ref.shape[0]
    j1 = jnp.minimum(i, nblocks - 1) % nb
    j2 = jnp.clip(i - 1, 0, nblocks - 1) % nb
    j3 = jnp.clip(i - 2, 0, nblocks - 1) % nb

    @pl.when(i == 0)
    def _():
        s_buf[...] = jnp.zeros(s_buf.shape, F32)
        p_buf[...] = jnp.zeros(p_buf.shape, BF16)
        den_buf[...] = jnp.ones(den_buf.shape, F32)

    def load_ctx(dst, src):
        low_ctx = lax.broadcasted_iota(jnp.int32, (n_ctx, w), 1) < 64
        for c in range(src.shape[1] // w):
            a, b = _dup_halves(src[:, w * c:w * (c + 1)], low_ctx)
            dst[3 * blk:, w * (2 * c):w * (2 * c + 1)] = a.astype(BF16)
            dst[3 * blk:, w * (2 * c + 1):w * (2 * c + 2)] = b.astype(BF16)

    @pl.when(j1 == 0)
    def _():
        load_ctx(kall, kctx_ref)

    @pl.when(j3 == 0)
    def _():
        load_ctx(vall, vctx_ref)

    def step(slot):
        low = lax.broadcasted_iota(jnp.int32, (nq, w), 1) < 64
        for r, src in enumerate((vp_ref, vc_ref, vn_ref)):
            vall[blk * r:blk * (r + 1), :] = src[...]
        for h in range(ATT_KV_HEADS):
            o = _dot(p_buf[1 - slot, h], vall[:, w * h:w * (h + 1)])
            for pair in range(ATT_GROUPS // 2):
                g0, g1 = 2 * pair, 2 * pair + 1
                o0 = o[nq * g0:nq * (g0 + 1)] / den_buf[1 - slot, h, nq * g0:nq * (g0 + 1), :]
                o1 = o[nq * g1:nq * (g1 + 1)] / den_buf[1 - slot, h, nq * g1:nq * (g1 + 1), :]
                col = (ATT_GROUPS * h) // 2 + pair
                o_ref[:, w * col:w * (col + 1)] = jnp.where(low, o0, o1).astype(BF16)
        row = lax.broadcasted_iota(jnp.int32, (blk, blk), 0)
        colm = lax.broadcasted_iota(jnp.int32, (blk, blk), 1)
        bias_prev = jnp.where(jnp.logical_and(colm >= row, j2 > 0), 0.0, NEG_INF).astype(F32)
        bias_next = jnp.where(jnp.logical_and(colm <= row, j2 < nb - 1), 0.0, NEG_INF).astype(F32)
        bias = jnp.concatenate([bias_prev, jnp.zeros((blk, blk), F32), bias_next,
                                jnp.zeros((blk, n_ctx), F32)], axis=1)
        for h in range(ATT_KV_HEADS):
            for g in range(ATT_GROUPS):
                rows = slice(nq * g, nq * (g + 1))
                sg = s_buf[1 - slot, h, rows, :] + bias
                snk = sink_ref[ATT_GROUPS * h + g] * LOG2E
                m = jnp.maximum(jnp.max(sg, axis=-1, keepdims=True), snk)
                p = jnp.exp2(sg - m)
                den = jnp.sum(p, axis=-1, keepdims=True) + jnp.exp2(snk - m)
                p_buf[slot, h, rows, :] = p.astype(BF16)
                den_buf[slot, h, rows, :] = jnp.broadcast_to(den, (nq, w))
        for r, src in enumerate((kp_ref, kc_ref, kn_ref)):
            kall[blk * r:blk * (r + 1), :] = src[...]
        for h in range(ATT_KV_HEADS):
            qs = []
            for g in range(ATT_GROUPS):
                a = ATT_GROUPS * h + g
                piece = q_ref[:, w * (a // 2):w * (a // 2 + 1)]
                qs.append(jnp.where(low if a % 2 == 0 else jnp.logical_not(low), piece, jnp.zeros_like(piece)))
            s_buf[slot, h] = _dot_nt(jnp.concatenate(qs, axis=0), kall[:, w * h:w * (h + 1)])

    @pl.when(i % 2 == 0)
    def _():
        step(0)

    @pl.when(i % 2 == 1)
    def _():
        step(1)


def _ret_kernel(lg_ref, q_ref, k_ref, v_ref, g_ref, *rest, t_len, c_len, prompt):
    if prompt:
        gn_ref, o_ref, sfin_ref, obuf, sf, sb = rest
    else:
        s0_ref, gn_ref, o_ref, obuf, sf, sb = rest
    nc = t_len // c_len
    h = pl.program_id(1)
    lgf = lg_ref[0, h]
    lgb = lg_ref[1, h]
    row = lax.broadcasted_iota(jnp.int32, (c_len, c_len), 0).astype(F32)
    col = lax.broadcasted_iota(jnp.int32, (c_len, c_len), 1).astype(F32)
    diff = row - col
    decay_f = jnp.where(diff >= 0, jnp.exp(lgf * jnp.maximum(diff, 0.0)), 0.0)
    decay_b = jnp.where(diff < 0, jnp.exp(lgb * jnp.maximum(-diff, 0.0)), 0.0)
    idx = lax.broadcasted_iota(jnp.int32, (c_len, 1), 0).astype(F32)
    one = jnp.ones((1, 1), F32)
    wq_f, ws_f, gc_f = jnp.exp(lgf * (idx + 1.0)), jnp.exp(lgf * (c_len - 1.0 - idx)), jnp.exp(lgf * c_len * one)
    wq_b, ws_b, gc_b = jnp.exp(lgb * (c_len - idx)), jnp.exp(lgb * idx), jnp.exp(lgb * c_len * one)
    if prompt:
        sf[...] = jnp.zeros(sf.shape, F32)
        sb[...] = jnp.zeros(sb.shape, F32)
    else:
        sf[...] = s0_ref[0]
        sb[...] = s0_ref[1]

    def rows(c):
        start = c * c_len
        return pl.ds(start if isinstance(start, int) else pl.multiple_of(start, c_len), c_len)

    def chunk(sl, state, decay, wq, ws, gc):
        qc, kc, vc = q_ref[sl, :], k_ref[sl, :], v_ref[sl, :]
        s_prev = state[...]
        sc = (_dot_nt(qc, kc) * decay).astype(BF16)
        o = _dot(sc, vc) + _dot(qc, s_prev.astype(BF16)) * wq
        kw = (kc.astype(F32) * ws).astype(BF16)
        state[...] = gc * s_prev + _dot_tn(kw, vc)
        return o

    def fwd(sl):
        return chunk(sl, sf, decay_f, wq_f, ws_f, gc_f)

    def bwd(sl):
        return chunk(sl, sb, decay_b, wq_b, ws_b, gc_b)

    def finish(sl, o):
        mu = jnp.mean(o, axis=-1, keepdims=True)
        dev = o - mu
        var = jnp.mean(dev * dev, axis=-1, keepdims=True)
        on = dev * lax.rsqrt(var + RMS_EPS) * gn_ref[...]
        o_ref[sl, :] = (g_ref[sl, :].astype(F32) * on).astype(BF16)

    half = nc // 2

    def first_half(i, carry):
        sl_f, sl_b = rows(i), rows(nc - 1 - i)
        obuf[sl_f, :] = fwd(sl_f)
        obuf[sl_b, :] = bwd(sl_b)
        return carry

    lax.fori_loop(0, half, first_half, 0)
    if nc % 2:
        sl = rows(half)
        finish(sl, fwd(sl) + bwd(sl))

    def second_half(i, carry):
        sl_f, sl_b = rows(i), rows(nc - 1 - i)
        finish(sl_f, fwd(sl_f) + obuf[sl_f, :])
        finish(sl_b, bwd(sl_b) + obuf[sl_b, :])
        return carry

    lax.fori_loop(half + nc % 2, nc, second_half, 0)
    if prompt:
        sfin_ref[0] = sf[...]
        sfin_ref[1] = sb[...]


def _softplus(x):
    return jnp.maximum(x, 0.0) + jnp.log1p(jnp.exp(-jnp.abs(x)))


def _two_steps(h, a0, u0, a1, u1):
    return a0 * h + u0, (a1 * a0) * h + (a1 * u0 + u1)


def _lru_kernel(xr_ref, prev_ref, next_ref, act_ref, cw_ref, cb_ref, wg_ref, br_ref, bi_ref, lam_ref,
                h0_ref, y_ref, hfin_ref, hf3, xc_buf, a3, u3, hb3, hc, *, nc):
    tc, d_rnn = xr_ref.shape
    bw = d_rnn // LRU_BLOCKS
    groups = tc // V7X_SUBLANES
    s = pl.program_id(1)
    is_b = s >= nc
    c = jnp.where(is_b, 2 * nc - 1 - s, s)
    base = pl.multiple_of(c * tc, tc)
    g0 = c * groups

    @pl.when(jnp.logical_not(is_b))
    def _():
        x = xr_ref[...]
        rowi = lax.broadcasted_iota(jnp.int32, (tc, d_rnn), 0)
        p1 = jnp.where(c > 0, prev_ref[V7X_SUBLANES - 1:V7X_SUBLANES, :], 0.0)
        p2 = jnp.where(c > 0, prev_ref[V7X_SUBLANES - 2:V7X_SUBLANES - 1, :], 0.0)
        n1 = jnp.where(c < nc - 1, next_ref[0:1, :], 0.0)
        xm1 = jnp.where(rowi == 0, p1, pltpu.roll(x, 1, 0))
        xm2 = jnp.where(rowi == 0, p2, jnp.where(rowi == 1, p1, pltpu.roll(x, 2, 0)))
        xp1 = jnp.where(rowi == tc - 1, n1, pltpu.roll(x, tc - 1, 0))
        xc_buf[pl.ds(base, tc), :] = (cw_ref[0:1, :] * xm2 + cw_ref[1:2, :] * xm1 + cw_ref[2:3, :] * x
                                      + cw_ref[3:4, :] * xp1 + cb_ref[...])

    xc = xc_buf[pl.ds(base, tc), :]
    xcb = xc.astype(BF16)
    c_row = (-0.5 * LRU_C) * _softplus(-lam_ref[...])
    half_xc = 0.5 * xc
    for n in range(LRU_BLOCKS):
        cs = slice(bw * n, bw * (n + 1))
        z = _dot(xcb[:, cs], wg_ref[n])
        tr = jnp.tanh(z[:, :bw] + br_ref[:, cs])
        ti = jnp.tanh(z[:, bw:] + bi_ref[:, cs])
        log_a = c_row[:, cs] * (tr + 1.0)
        a = jnp.exp(log_a)
        u = jnp.sqrt(-jnp.tanh(log_a) * (a * a + 1.0)) * ((ti + 1.0) * half_xc[:, cs])
        a3[:, :, cs] = a.reshape(groups, V7X_SUBLANES, bw)
        u3[:, :, cs] = u.reshape(groups, V7X_SUBLANES, bw)


    @pl.when(s == 0)
    def _():
        hc[...] = h0_ref[0:1, :]

    @pl.when(s == nc)
    def _():
        hc[...] = h0_ref[1:2, :]

    @pl.when(jnp.logical_not(is_b))
    def _():
        def group(i, h):
            for r in range(0, V7X_SUBLANES, 2):
                h0, h = _two_steps(h, a3[i, pl.ds(r, 1), :], u3[i, pl.ds(r, 1), :],
                                   a3[i, pl.ds(r + 1, 1), :], u3[i, pl.ds(r + 1, 1), :])
                hf3[g0 + i, pl.ds(r, 1), :] = h0
                hf3[g0 + i, pl.ds(r + 1, 1), :] = h
            return h
        h = lax.fori_loop(0, groups, group, hc[...])
        hc[...] = h

        @pl.when(s == nc - 1)
        def _():
            hfin_ref[0:1, :] = h

    @pl.when(is_b)
    def _():
        def group(k, h):
            i = groups - 1 - k
            for r in range(V7X_SUBLANES - 1, 0, -2):
                h0, h = _two_steps(h, a3[i, pl.ds(r, 1), :], u3[i, pl.ds(r, 1), :],
                                   a3[i, pl.ds(r - 1, 1), :], u3[i, pl.ds(r - 1, 1), :])
                hb3[i, pl.ds(r, 1), :] = h0
                hb3[i, pl.ds(r - 1, 1), :] = h
            return h
        h = lax.fori_loop(0, groups, group, hc[...])
        hc[...] = h
        rec = (hf3[pl.ds(g0, groups)] + hb3[...]).reshape(tc, d_rnn)
        y_ref[...] = (act_ref[...] * rec).astype(BF16)

        @pl.when(s == 2 * nc - 1)
        def _():
            hfin_ref[1:2, :] = h


def _out_mlp_kernel(*refs, n_prompt, split_x, split_out):
    x, rest = _read_x(refs, n_prompt, split_x)
    yp_ref, ys_ref, mod_ref, gm_ref, wo_ref, wu_ref, wd_ref = rest[:7]
    d = x.shape[1]
    d_ff = wu_ref.shape[1]
    is_prompt = pl.program_id(0) < n_prompt
    y = jnp.where(is_prompt, yp_ref[...], ys_ref[...])
    x1 = x + mod_ref[:, 2 * d:3 * d] * _dot(y, wo_ref[...])
    hn = _modulate(x1, gm_ref[...], mod_ref[:, 3 * d:4 * d], mod_ref[:, 4 * d:5 * d]).astype(BF16)
    acc = jnp.zeros(x1.shape, F32)
    for c in range(d_ff // FF_CHUNK):
        hh = jnp.maximum(_dot(hn, wu_ref[:, FF_CHUNK * c:FF_CHUNK * (c + 1)]), 0.0)
        acc = acc + _dot((hh * hh).astype(BF16), wd_ref[FF_CHUNK * c:FF_CHUNK * (c + 1), :])
    out = x1 + mod_ref[:, 5 * d:6 * d] * acc
    if split_out:
        op_ref, os_ref = rest[7:]

        @pl.when(is_prompt)
        def _():
            op_ref[...] = out

        @pl.when(jnp.logical_not(is_prompt))
        def _():
            os_ref[...] = out
    else:
        rest[7][...] = out


class _Dims:
    def __init__(self, bp, lp, bs, ts, d):
        assert bp * lp == ts, "prompt tokens must fill exactly one segment"
        self.bp, self.lp, self.bs, self.ts, self.d = bp, lp, bs, ts, d
        self.nseg = 1 + bs
        self.n = self.nseg * ts
        self.tm = min(TOKEN_TILE, ts)
        assert ts % self.tm == 0 and ts % ATT_BLOCK == 0
        assert ts % RET_KERNEL_CHUNK == 0 and lp % min(RET_KERNEL_CHUNK, lp) == 0
        self.tps = ts // self.tm
        self.ntiles = self.n // self.tm


def _in_kernel(body, dm, split_x):
    return functools.partial(body, n_prompt=dm.tps, split_x=split_x)


def _attn_layer(dm, layer, slot, xs, norm_mix, mods, w_in, q_gain, k_gain, sink, cache_k, cache_v, rope):
    d, tm, tps, n = dm.d, dm.tm, dm.tps, dm.n
    split_x = len(xs) == 2
    hd = ATT_HEAD_DIM
    q_w, kv_w = ATT_HEADS * hd, ATT_KV_HEADS * hd
    gain = jnp.concatenate([jnp.tile(q_gain, ATT_HEADS), jnp.tile(k_gain, ATT_KV_HEADS)])[None, :]
    gw = q_w + kv_w
    assert (q_w + kv_w) % gw == 0 and gw % V7X_LANES == 0 and gw // hd <= V7X_LANES
    head_of_lane = np.arange(gw) // hd
    hsum = jnp.asarray(head_of_lane[:, None] == np.arange(V7X_LANES)[None, :], BF16)
    hexp = jnp.asarray(np.arange(V7X_LANES)[:, None] == head_of_lane[None, :], BF16)
    cos, sin = rope
    nt = dm.ntiles
    cur = lambda i: jnp.minimum(i, nt - 1)
    prev = lambda i: jnp.maximum(i - 1, 0)
    tok = lambda w: pl.BlockSpec((tm, w), lambda i: (prev(i), 0))
    q, k, v, kf, vf = pl.pallas_call(
        _in_kernel(_attn_in_kernel, dm, split_x),
        grid=(nt + 1,),
        in_specs=_token_specs(tm, tps, d, layer, split_x, cur) + [
            _resident(w_in.shape), _resident(gain.shape)] + _rope_specs(tm, tps, V7X_LANES, prev) + [
            _resident(hsum.shape), _resident(hexp.shape)],
        out_specs=[tok(q_w), tok(2 * kv_w), tok(2 * kv_w), tok(kv_w), tok(kv_w)],
        out_shape=[jax.ShapeDtypeStruct((n, q_w), BF16), jax.ShapeDtypeStruct((n, 2 * kv_w), BF16),
                   jax.ShapeDtypeStruct((n, 2 * kv_w), BF16), jax.ShapeDtypeStruct((n, kv_w), F32),
                   jax.ShapeDtypeStruct((n, kv_w), F32)],
        scratch_shapes=[pltpu.VMEM((2, tm, q_w + 2 * kv_w), F32)],
        compiler_params=_cparams(("arbitrary",)),
        name="attn_in",
    )(*xs, norm_mix, mods, w_in, gain, cos, sin, hsum, hexp)
    kf, vf = kf[:dm.ts], vf[:dm.ts]

    smem = pl.BlockSpec(memory_space=pltpu.SMEM)
    lp, bp, bs, ts = dm.lp, dm.bp, dm.bs, dm.ts
    y_p = pl.pallas_call(
        _attn_prompt_kernel,
        grid=(bp,),
        in_specs=[smem, pl.BlockSpec((lp, q_w), lambda b: (b, 0)),
                  pl.BlockSpec((lp, 2 * kv_w), lambda b: (b, 0)),
                  pl.BlockSpec((lp, 2 * kv_w), lambda b: (b, 0))],
        out_specs=pl.BlockSpec((lp, q_w), lambda b: (b, 0)),
        out_shape=jax.ShapeDtypeStruct((ts, q_w), BF16),
        compiler_params=_cparams(("arbitrary",)),
        name="attn_prompt",
    )(sink, q, k, v)

    blk = ATT_BLOCK
    nb = ts // blk
    nblocks = bs * nb
    past = cache_k.shape[2]
    nk = 3 * blk + past
    c1 = lambda i: jnp.minimum(i, nblocks - 1)
    c3 = lambda i: jnp.clip(i - 2, 0, nblocks - 1)
    cur = lambda c: lambda i: (nb + c(i), 0)
    prv = lambda c: lambda i: (nb + c(i) - jnp.where(c(i) % nb > 0, 1, 0), 0)
    nxt = lambda c: lambda i: (nb + c(i) + jnp.where(c(i) % nb < nb - 1, 1, 0), 0)
    kv_spec = lambda f: pl.BlockSpec((blk, 2 * kv_w), f)
    ctx_spec = lambda c: pl.BlockSpec((None, None, past, kv_w), lambda i: (c(i) // nb, slot, 0, 0))
    rows = ATT_GROUPS * blk
    y_s = pl.pallas_call(
        functools.partial(_attn_sample_kernel, nb=nb, nblocks=nblocks),
        grid=(nblocks + 2,),
        in_specs=[smem, pl.BlockSpec((blk, q_w), cur(c1)),
                  kv_spec(prv(c1)), kv_spec(cur(c1)), kv_spec(nxt(c1)),
                  kv_spec(prv(c3)), kv_spec(cur(c3)), kv_spec(nxt(c3)), ctx_spec(c1), ctx_spec(c3)],
        out_specs=pl.BlockSpec((blk, q_w), lambda i: (c3(i), 0)),
        out_shape=jax.ShapeDtypeStruct((bs * ts, q_w), BF16),
        scratch_shapes=[pltpu.VMEM((nk, 2 * kv_w), BF16), pltpu.VMEM((nk, 2 * kv_w), BF16),
                        pltpu.VMEM((2, ATT_KV_HEADS, rows, nk), F32),
                        pltpu.VMEM((2, ATT_KV_HEADS, rows, nk), BF16),
                        pltpu.VMEM((2, ATT_KV_HEADS, rows, V7X_LANES), F32)],
        compiler_params=_cparams(("arbitrary",)),
        name="attn_sample",
    )(sink, q, k, k, k, v, v, v, cache_k, cache_v)
    return y_p, y_s, kf, vf


def _ret_layer(dm, layer, slot, xs, norm_mix, mods, w_in, gn_gain, log_decay, state, rope):
    d, tm, tps, n = dm.d, dm.tm, dm.tps, dm.n
    split_x = len(xs) == 2
    e1, ev = RET_HEADS * RET_DK, RET_HEADS * RET_DV
    cos, sin = rope
    tok = lambda w: pl.BlockSpec((tm, w), lambda i: (i, 0))
    q, k, v, g = pl.pallas_call(
        _in_kernel(_ret_in_kernel, dm, split_x),
        grid=(dm.ntiles,),
        in_specs=(_token_specs(tm, tps, d, layer, split_x) + [_resident(w_in.shape)]
                  + _rope_specs(tm, tps, RET_DK)),
        out_specs=[tok(e1), tok(e1), tok(ev), tok(ev)],
        out_shape=[jax.ShapeDtypeStruct((n, e1), BF16), jax.ShapeDtypeStruct((n, e1), BF16),
                   jax.ShapeDtypeStruct((n, ev), BF16), jax.ShapeDtypeStruct((n, ev), BF16)],
        compiler_params=_cparams(("arbitrary",)),
        name="ret_in",
    )(*xs, norm_mix, mods, w_in, cos, sin)

    smem = pl.BlockSpec(memory_space=pltpu.SMEM)
    gn = gn_gain[None, :]
    lp, bp, bs, ts = dm.lp, dm.bp, dm.bs, dm.ts
    state_spec = lambda f: pl.BlockSpec((None, None, 2, None, RET_DK, RET_DV), f)

    def call(t_len, nbatch, seg0, prompt, name):
        rows = lambda w: pl.BlockSpec((t_len, w), lambda b, h: (seg0 + b, h))
        in_specs = [smem, rows(RET_DK), rows(RET_DK), rows(RET_DV), rows(RET_DV)]
        args = [log_decay, q, k, v, g]
        out_shape = [jax.ShapeDtypeStruct((nbatch * t_len, ev), BF16)]
        out_specs = [pl.BlockSpec((t_len, RET_DV), lambda b, h: (b, h))]
        if prompt:
            out_shape.append(jax.ShapeDtypeStruct((nbatch, 1, 2, RET_HEADS, RET_DK, RET_DV), F32))
            out_specs.append(state_spec(lambda b, h: (b, 0, 0, h, 0, 0)))
        else:
            in_specs.append(state_spec(lambda b, h: (b, slot, 0, h, 0, 0)))
            args.append(state)
        in_specs.append(pl.BlockSpec((1, RET_DV), lambda b, h: (0, h)))
        args.append(gn)
        return pl.pallas_call(
            functools.partial(_ret_kernel, t_len=t_len, c_len=min(RET_KERNEL_CHUNK, t_len), prompt=prompt),
            grid=(nbatch, RET_HEADS),
            in_specs=in_specs, out_specs=out_specs, out_shape=out_shape,
            scratch_shapes=[pltpu.VMEM((t_len, RET_DV), F32), pltpu.VMEM((RET_DK, RET_DV), F32),
                            pltpu.VMEM((RET_DK, RET_DV), F32)],
            compiler_params=_cparams(("arbitrary", "arbitrary")),
            name=name,
        )(*args)

    y_p, s_new = call(lp, bp, 0, True, "ret_prompt")
    (y_s,) = call(ts, bs, 1, False, "ret_sample")
    return y_p, y_s, s_new


def _lru_layer(dm, layer, slot, xs, norm_mix, mods, w_in, conv_w, conv_b, w_r, b_r, w_i, b_i, lam, state):
    d, tm, tps, n = dm.d, dm.tm, dm.tps, dm.n
    split_x = len(xs) == 2
    d_rnn = w_in.shape[1] // 2
    tok = lambda w: pl.BlockSpec((tm, w), lambda i: (i, 0))
    act, xr = pl.pallas_call(
        _in_kernel(_lru_in_kernel, dm, split_x),
        grid=(dm.ntiles,),
        in_specs=_token_specs(tm, tps, d, layer, split_x) + [_resident(w_in.shape)],
        out_specs=[tok(d_rnn), tok(d_rnn)],
        out_shape=[jax.ShapeDtypeStruct((n, d_rnn), F32), jax.ShapeDtypeStruct((n, d_rnn), F32)],
        compiler_params=_cparams(("arbitrary",)),
        name="lru_in",
    )(*xs, norm_mix, mods, w_in)

    wg = (0.5 * jnp.concatenate([w_r, w_i], axis=-1)).astype(BF16)
    b_r, b_i = 0.5 * b_r, 0.5 * b_i
    vec = lambda a: a[:, None, :]
    lp, bp, bs, ts = dm.lp, dm.bp, dm.bs, dm.ts
    sub = V7X_SUBLANES

    def call(t_len, nbatch, row0, h0, h0_slot, name):
        tc = t_len if t_len <= 256 else min(512, t_len // 2)
        nc = t_len // tc
        cb = row0 // tc
        early = lambda s: jnp.minimum(s, nc - 1)
        late = lambda s: jnp.where(s >= nc, 2 * nc - 1 - s, nc - 1)
        dirv = lambda s: jnp.where(s >= nc, 1, 0)
        r8 = tc // sub
        first8 = lambda b: (row0 + b * t_len) // sub
        prev8 = lambda b, s: (jnp.maximum(first8(b) + early(s) * r8 - 1, first8(b)), 0)
        next8 = lambda b, s: (jnp.minimum(first8(b) + (early(s) + 1) * r8, first8(b) + t_len // sub - 1), 0)
        dspec = lambda shape: pl.BlockSpec((None,) + shape, lambda b, s: (dirv(s),) + (0,) * len(shape))
        return pl.pallas_call(
            functools.partial(_lru_kernel, nc=nc),
            grid=(nbatch, 2 * nc),
            in_specs=[pl.BlockSpec((tc, d_rnn), lambda b, s: (cb + b * nc + early(s), 0)),
                      pl.BlockSpec((sub, d_rnn), prev8), pl.BlockSpec((sub, d_rnn), next8),
                      pl.BlockSpec((tc, d_rnn), lambda b, s: (cb + b * nc + late(s), 0)),
                      pl.BlockSpec((CONV_W, d_rnn), lambda b, s: (0, 0)),
                      pl.BlockSpec((1, d_rnn), lambda b, s: (0, 0)),
                      dspec(wg.shape[1:]), dspec((1, d_rnn)), dspec((1, d_rnn)), dspec((1, d_rnn)),
                      pl.BlockSpec((None, None, 2, d_rnn), lambda b, s: (b, h0_slot, 0, 0))],
            out_specs=[pl.BlockSpec((tc, d_rnn), lambda b, s: (b * nc + late(s), 0)),
                       pl.BlockSpec((None, 2, d_rnn), lambda b, s: (b, 0, 0))],
            out_shape=[jax.ShapeDtypeStruct((nbatch * t_len, d_rnn), BF16),
                       jax.ShapeDtypeStruct((nbatch, 2, d_rnn), F32)],
            scratch_shapes=[pltpu.VMEM((t_len // sub, sub, d_rnn), F32), pltpu.VMEM((t_len, d_rnn), F32),
                            pltpu.VMEM((tc // sub, sub, d_rnn), F32), pltpu.VMEM((tc // sub, sub, d_rnn), F32),
                            pltpu.VMEM((tc // sub, sub, d_rnn), F32), pltpu.VMEM((1, d_rnn), F32)],
            compiler_params=_cparams(("arbitrary", "arbitrary")),
            name=name,
        )(xr, xr, xr, act, conv_w, conv_b[None, :], wg, vec(b_r), vec(b_i), vec(lam), h0)

    y_p, h_new = call(lp, bp, 0, jnp.zeros((bp, 1, 2, d_rnn), F32), 0, "lru_prompt")
    y_s, _ = call(ts, bs, ts, state, slot, "lru_sample")
    return y_p, y_s, h_new


def _out_mlp(dm, layer, xs, y_p, y_s, norm_mlp, mods, w_out, w_up, w_down, split_out):
    d, tm, tps, n = dm.d, dm.tm, dm.tps, dm.n
    split_x = len(xs) == 2
    din = w_out.shape[0]
    first = lambda i: (jnp.minimum(i, tps - 1), 0)
    second = lambda i: (jnp.maximum(i - tps, 0), 0)
    if split_out:
        out_specs = [pl.BlockSpec((tm, d), first), pl.BlockSpec((tm, d), second)]
        out_shape = [jax.ShapeDtypeStruct((dm.ts, d), F32), jax.ShapeDtypeStruct((n - dm.ts, d), F32)]
    else:
        out_specs = pl.BlockSpec((tm, d), lambda i: (i, 0))
        out_shape = jax.ShapeDtypeStruct((n, d), F32)
    return pl.pallas_call(
        functools.partial(_out_mlp_kernel, n_prompt=tps, split_x=split_x, split_out=split_out),
        grid=(dm.ntiles,),
        in_specs=_x_specs(tm, tps, d, split_x) + [
            pl.BlockSpec((tm, din), first), pl.BlockSpec((tm, din), second),
            pl.BlockSpec((None, None, 1, N_MOD * d), lambda i: (layer, i // tps, 0, 0)),
            pl.BlockSpec((None, 1, d), lambda i: (layer, 0, 0)),
            _resident(w_out.shape), _layer_resident(w_up.shape, layer), _layer_resident(w_down.shape, layer)],
        out_specs=out_specs, out_shape=out_shape,
        compiler_params=_cparams(("arbitrary",)),
        name="out_mlp",
    )(*xs, y_p, y_s, mods, norm_mlp, w_out, w_up, w_down)


def kernel(x_prompt, x_sample, cache_attn_k, cache_attn_v, state_ret, state_lru, c, c_ctx, norm_mix, norm_mlp, w_ada, b_ada, w_up, w_down, attn_w_in, attn_w_out, attn_q_gain, attn_k_gain, attn_sink, ret_w_in, ret_w_out, ret_gn_gain, ret_log_decay, lru_w_in, lru_conv_w, lru_conv_b, lru_w_r, lru_b_r, lru_w_i, lru_b_i, lru_lambda, lru_w_out):
    bp, lp, d = x_prompt.shape
    bs, ts, _ = x_sample.shape
    dm = _Dims(bp, lp, bs, ts, d)
    depth = w_ada.shape[0]
    assert dm.nseg <= MOD_ROWS

    xs = (x_prompt.reshape(ts, d), x_sample.reshape(bs * ts, d))
    cvec = jnp.concatenate([c_ctx[None, :], c, jnp.zeros((MOD_ROWS - dm.nseg, d), F32)], axis=0)
    mods = _modulation(cvec, w_ada, b_ada)[:, :dm.nseg].reshape(depth, dm.nseg, 1, N_MOD * d)
    nmix = norm_mix[:, None, :]
    nmlp = norm_mlp[:, None, :]

    rope_attn = tuple(jnp.tile(t, (1, 1, 2)) for t in _rope_tables(ts, ATT_HEAD_DIM // 2))
    rope_ret = _rope_tables(ts, RET_DK // 2)
    kv_w = ATT_KV_HEADS * ATT_HEAD_DIM
    past = cache_attn_k.shape[2]
    ck = cache_attn_k.reshape(bs, -1, past, kv_w)
    cv = cache_attn_v.reshape(bs, -1, past, kv_w)

    w_up_b, w_down_b = w_up.astype(BF16), w_down.astype(BF16)

    new_k, new_v, new_ret, new_lru = [], [], [], []
    for layer in range(depth):
        kind, slot = layer % 3, layer // 3
        if kind == 0:
            y_p, y_s, kf, vf = _attn_layer(dm, layer, slot, xs, nmix, mods, attn_w_in[slot].astype(BF16),
                                           attn_q_gain[slot], attn_k_gain[slot], attn_sink[slot],
                                           ck, cv, rope_attn)
            new_k.append(kf.reshape(bp, lp, ATT_KV_HEADS, ATT_HEAD_DIM))
            new_v.append(vf.reshape(bp, lp, ATT_KV_HEADS, ATT_HEAD_DIM))
            w_out = attn_w_out[slot]
        elif kind == 1:
            y_p, y_s, s_new = _ret_layer(dm, layer, slot, xs, nmix, mods, ret_w_in[slot].astype(BF16),
                                         ret_gn_gain[slot], ret_log_decay[slot], state_ret, rope_ret)
            new_ret.append(s_new)
            w_out = ret_w_out[slot]
        else:
            y_p, y_s, h_new = _lru_layer(dm, layer, slot, xs, nmix, mods, lru_w_in[slot].astype(BF16),
                                         lru_conv_w[slot], lru_conv_b[slot], lru_w_r[slot], lru_b_r[slot],
                                         lru_w_i[slot], lru_b_i[slot], lru_lambda[slot], state_lru)
            new_lru.append(h_new)
            w_out = lru_w_out[slot]
        last = layer == depth - 1
        out = _out_mlp(dm, layer, xs, y_p, y_s, nmlp, mods, w_out.astype(BF16), w_up_b, w_down_b,
                       split_out=last)
        xs = tuple(out) if last else (out,)

    y_prompt = xs[0].reshape(bp, lp, d)
    y_sample = xs[1].reshape(bs, ts, d)
    return (y_prompt, y_sample, jnp.stack(new_k, axis=1), jnp.stack(new_v, axis=1),
            jnp.concatenate(new_ret, axis=1), jnp.stack(new_lru, axis=1))
```

```python
import functools

import jax
import jax.numpy as jnp
import numpy as np
from jax import lax
from jax.experimental import pallas as pl
from jax.experimental.pallas import tpu as pltpu

F32 = jnp.float32
BF16 = jnp.bfloat16

N_MOD = 6
RMS_EPS = 1e-6
ROPE_BASE = 10000.0
NEG_INF = -1e30
LOG2E = 1.4426950408889634
GRID_W = 64
ATT_HEADS = 16
ATT_KV_HEADS = 4
ATT_HEAD_DIM = 64
ATT_GROUPS = ATT_HEADS // ATT_KV_HEADS
ATT_BLOCK = 128
RET_HEADS = 4
RET_DK = 256
RET_DV = 512
LRU_BLOCKS = 8
CONV_W = 4
LRU_C = 8.0

V7X_LANES = 128
V7X_SUBLANES = 8
V7X_VMEM_BYTES = 64 * 1024 * 1024
VMEM_LIMIT = V7X_VMEM_BYTES - 8 * 1024 * 1024

TOKEN_TILE = 512
MOD_ROWS = 16
FF_CHUNK = 1024
RET_KERNEL_CHUNK = 256


def _cparams(sem):
    return pltpu.CompilerParams(dimension_semantics=sem, vmem_limit_bytes=VMEM_LIMIT)


def _resident(shape):
    zeros = (0,) * len(shape)
    return pl.BlockSpec(shape, lambda *_: zeros, pipeline_mode=pl.Buffered(1))


def _layer_resident(shape, layer):
    zeros = (0,) * (len(shape) - 1)
    return pl.BlockSpec((None,) + tuple(shape[1:]), lambda *_: (layer,) + zeros, pipeline_mode=pl.Buffered(1))


def _dot(a, b):
    return jnp.dot(a, b, preferred_element_type=F32)


def _dot_nt(a, b):
    return lax.dot_general(a, b, (((1,), (1,)), ((), ())), preferred_element_type=F32)


def _dot_tn(a, b):
    return lax.dot_general(a, b, (((0,), (0,)), ((), ())), preferred_element_type=F32)


def _silu(x):
    return x * jax.nn.sigmoid(x)


def _modulate(x, g, shift, scale):
    ms = jnp.mean(x * x, axis=-1, keepdims=True)
    y = x * lax.rsqrt(ms + RMS_EPS) * g
    return y * (1.0 + scale) + shift


def _split_bf16(x):
    hi = x.astype(BF16)
    lo = (x - hi.astype(F32)).astype(BF16)
    return hi, lo


def _read_x(refs, n_prompt, split_x):
    if split_x:
        x = jnp.where(pl.program_id(0) < n_prompt, refs[0][...], refs[1][...])
        return x, refs[2:]
    return refs[0][...], refs[1:]


def _mod_kernel(c_ref, w_ref, b_ref, o_ref):
    s = _silu(c_ref[...]).astype(BF16)
    o_ref[...] = _dot(s, w_ref[...].astype(BF16)) + b_ref[...]


def _modulation(cvec, w_ada, b_ada):
    depth, d, w = w_ada.shape
    tn = 1536
    return pl.pallas_call(
        _mod_kernel,
        grid=(depth, w // tn),
        in_specs=[pl.BlockSpec((MOD_ROWS, d), lambda l, j: (0, 0)),
                  pl.BlockSpec((None, d, tn), lambda l, j: (l, 0, j)),
                  pl.BlockSpec((None, 1, tn), lambda l, j: (l, 0, j))],
        out_specs=pl.BlockSpec((None, MOD_ROWS, tn), lambda l, j: (l, 0, j)),
        out_shape=jax.ShapeDtypeStruct((depth, MOD_ROWS, w), F32),
        compiler_params=_cparams(("arbitrary", "arbitrary")),
        name="modulation",
    )(cvec, w_ada, b_ada.reshape(depth, 1, w))


def _rope_tables(t_len, sect):
    half = sect // 2
    inv = ROPE_BASE ** (-jnp.arange(half, dtype=F32) / half)
    t = jnp.arange(t_len, dtype=jnp.int32)

    def one(pos):
        ang = pos.astype(F32)[:, None] * inv[None, :]
        c, s = jnp.cos(ang), jnp.sin(ang)
        return jnp.concatenate([c, c], axis=1), jnp.concatenate([-s, s], axis=1)

    cr, sr = one(t // GRID_W)
    cc, sc = one(t % GRID_W)
    cos = jnp.concatenate([cr, cc], axis=1)
    sin = jnp.concatenate([sr, sc], axis=1)
    return (jnp.stack([jnp.ones_like(cos), cos]), jnp.stack([jnp.zeros_like(sin), sin]))


def _dup_halves(x, lane_low):
    r = pltpu.roll(x, 64, 1)
    return jnp.where(lane_low, x, r), jnp.where(lane_low, r, x)


def _head_halves(x, lane_low, dup):
    r = pltpu.roll(x, 64, 1)
    return jnp.where(lane_low, x, jnp.where(dup, r, 1.0)), jnp.where(lane_low, r, jnp.where(dup, x, 1.0))


def _attn_in_kernel(*refs, n_prompt, split_x):
    x, rest = _read_x(refs, n_prompt, split_x)
    (g_ref, mod_ref, w_ref, gain_ref, cos_ref, sin_ref, hsum_ref, hexp_ref,
     q_out, k_out, v_out, kf_out, vf_out, proj_s) = rest
    tm, d = x.shape
    i = pl.program_id(0)
    qk_w = hsum_ref.shape[0]
    n_q = ATT_HEADS * ATT_HEAD_DIM // V7X_LANES

    @pl.when(i == 0)
    def _():
        proj_s[1] = jnp.zeros(proj_s.shape[1:], F32)

    def epilogue(proj):
        lane = lax.broadcasted_iota(jnp.int32, (tm, V7X_LANES), 1)
        first = (lane % 32) < 16
        low = lane < 64
        cos = cos_ref[...]
        sin = sin_ref[...]
        qk = proj[:, :qk_w]
        ssum = _dot((qk * qk).astype(BF16), hsum_ref[...])
        r_hi, r_lo = _split_bf16(lax.rsqrt(ssum * (1.0 / ATT_HEAD_DIM) + RMS_EPS))
        qkn = qk * (_dot(r_hi, hexp_ref[...]) + _dot(r_lo, hexp_ref[...])) * gain_ref[...]
        for j in range(qk_w // V7X_LANES):
            xj = qkn[:, V7X_LANES * j:V7X_LANES * (j + 1)]
            rot = jnp.where(first, pltpu.roll(xj, V7X_LANES - 16, 1), pltpu.roll(xj, 16, 1))
            yj = xj * cos + rot * sin
            if j < n_q:
                q_out[:, V7X_LANES * j:V7X_LANES * (j + 1)] = (yj * (ATT_HEAD_DIM ** -0.5 * LOG2E)).astype(BF16)
            else:
                jj = j - n_q
                kf_out[:, V7X_LANES * jj:V7X_LANES * (jj + 1)] = yj
                a, b = _dup_halves(yj, low)
                k_out[:, V7X_LANES * (2 * jj):V7X_LANES * (2 * jj + 1)] = a.astype(BF16)
                k_out[:, V7X_LANES * (2 * jj + 1):V7X_LANES * (2 * jj + 2)] = b.astype(BF16)
        v = proj[:, qk_w:]
        vf_out[...] = v
        prompt_tile = i - 1 < n_prompt
        for jj in range(v.shape[1] // V7X_LANES):
            a, b = _head_halves(v[:, V7X_LANES * jj:V7X_LANES * (jj + 1)], low, prompt_tile)
            v_out[:, V7X_LANES * (2 * jj):V7X_LANES * (2 * jj + 1)] = a.astype(BF16)
            v_out[:, V7X_LANES * (2 * jj + 1):V7X_LANES * (2 * jj + 2)] = b.astype(BF16)

    def step(slot):
        epilogue(proj_s[1 - slot])
        h = _modulate(x, g_ref[...], mod_ref[:, 0:d], mod_ref[:, d:2 * d]).astype(BF16)
        proj_s[slot] = _dot(h, w_ref[...])

    @pl.when(i % 2 == 0)
    def _():
        step(0)

    @pl.when(i % 2 == 1)
    def _():
        step(1)


def _ret_in_kernel(*refs, n_prompt, split_x):
    x, (g_ref, mod_ref, w_ref, cos_ref, sin_ref, q_out, k_out, v_out, g_out) = _read_x(refs, n_prompt, split_x)
    d = x.shape[1]
    e1 = RET_HEADS * RET_DK
    e3 = 2 * e1 + RET_HEADS * RET_DV
    h = _modulate(x, g_ref[...], mod_ref[:, 0:d], mod_ref[:, d:2 * d]).astype(BF16)
    n_chunks = e1 // V7X_LANES
    for sec, out in ((0, q_out), (1, k_out)):
        p = _dot(h, w_ref[:, sec * e1:(sec + 1) * e1])
        for j in range(n_chunks):
            s = j % 2
            xj = p[:, V7X_LANES * j:V7X_LANES * (j + 1)]
            yj = (xj * cos_ref[:, V7X_LANES * s:V7X_LANES * (s + 1)]
                  + pltpu.roll(xj, 64, 1) * sin_ref[:, V7X_LANES * s:V7X_LANES * (s + 1)])
            if sec == 1:
                yj = yj * (RET_DK ** -0.5)
            out[:, V7X_LANES * j:V7X_LANES * (j + 1)] = yj.astype(BF16)
    v_out[...] = _dot(h, w_ref[:, 2 * e1:e3]).astype(BF16)
    g_out[...] = _silu(_dot(h, w_ref[:, e3:])).astype(BF16)


def _lru_in_kernel(*refs, n_prompt, split_x):
    x, (g_ref, mod_ref, w_ref, act_out, xr_out) = _read_x(refs, n_prompt, split_x)
    d = x.shape[1]
    d_rnn = act_out.shape[1]
    h = _modulate(x, g_ref[...], mod_ref[:, 0:d], mod_ref[:, d:2 * d]).astype(BF16)
    act_out[...] = jax.nn.gelu(_dot(h, w_ref[:, :d_rnn]), approximate=True)
    xr_out[...] = _dot(h, w_ref[:, d_rnn:])


def _x_specs(tm, tps, d, split_x, tile=lambda i: i):
    if split_x:
        return [pl.BlockSpec((tm, d), lambda i: (jnp.minimum(tile(i), tps - 1), 0)),
                pl.BlockSpec((tm, d), lambda i: (jnp.maximum(tile(i) - tps, 0), 0))]
    return [pl.BlockSpec((tm, d), lambda i: (tile(i), 0))]


def _token_specs(tm, tps, d, layer, split_x, tile=lambda i: i):
    return _x_specs(tm, tps, d, split_x, tile) + [
        pl.BlockSpec((None, 1, d), lambda i: (layer, 0, 0)),
        pl.BlockSpec((None, None, 1, N_MOD * d), lambda i: (layer, tile(i) // tps, 0, 0))]


def _rope_specs(tm, tps, width, tile=lambda i: i):
    spec = pl.BlockSpec((None, tm, width), lambda i: (jnp.minimum(tile(i) // tps, 1), tile(i) % tps, 0))
    return [spec, spec]


def _attend(q_ref, k_ref, v_ref, bias, sink_ref, o_ref):
    nq = q_ref.shape[0]
    w = V7X_LANES
    low = lax.broadcasted_iota(jnp.int32, (nq, w), 1) < 64
    for h in range(ATT_KV_HEADS):
        hcol = slice(w * h, w * (h + 1))
        qs = []
        for g in range(ATT_GROUPS):
            a = ATT_GROUPS * h + g
            piece = q_ref[:, w * (a // 2):w * (a // 2 + 1)]
            qs.append(jnp.where(low if a % 2 == 0 else jnp.logical_not(low), piece, jnp.zeros_like(piece)))
        s = _dot_nt(jnp.concatenate(qs, axis=0), k_ref[:, hcol])
        ps, dens = [], []
        for g in range(ATT_GROUPS):
            sg = s[nq * g:nq * (g + 1)]
            if bias is not None:
                sg = sg + bias
            snk = sink_ref[ATT_GROUPS * h + g] * LOG2E
            m = jnp.maximum(jnp.max(sg, axis=-1, keepdims=True), snk)
            p = jnp.exp2(sg - m)
            dens.append(jnp.sum(p, axis=-1, keepdims=True) + jnp.exp2(snk - m))
            ps.append(p.astype(BF16))
        o = _dot(jnp.concatenate(ps, axis=0), v_ref[:, hcol])
        for pair in range(ATT_GROUPS // 2):
            g0, g1 = 2 * pair, 2 * pair + 1
            o0 = o[nq * g0:nq * (g0 + 1)] / dens[g0]
            o1 = o[nq * g1:nq * (g1 + 1)] / dens[g1]
            col = (ATT_GROUPS * h) // 2 + pair
            o_ref[:, w * col:w * (col + 1)] = jnp.where(low, o0, o1).astype(BF16)


def _attn_prompt_kernel(sink_ref, q_ref, k_ref, v_ref, o_ref):
    _attend(q_ref, k_ref, v_ref, None, sink_ref, o_ref)


def _attn_sample_kernel(sink_ref, q_ref, kp_ref, kc_ref, kn_ref, vp_ref, vc_ref, vn_ref,
                        kctx_ref, vctx_ref, o_ref, kall, vall, s_buf, p_buf, e_buf, *, nb, nblocks):
    i = pl.program_id(0)
    blk, w = ATT_BLOCK, V7X_LANES
    nq = blk
    n_ctx = kctx_ref.shape[0]
    j1 = jnp.minimum(i, nblocks - 1) % nb
    j2 = jnp.clip(i - 1, 0, nblocks - 1) % nb
    j3 = jnp.clip(i - 2, 0, nblocks - 1) % nb

    @pl.when(i == 0)
    def _():
        s_buf[...] = jnp.zeros(s_buf.shape, F32)
        p_buf[...] = jnp.zeros(p_buf.shape, BF16)
        e_buf[...] = jnp.ones(e_buf.shape, F32)

    def load_ctx(dst, src, dup):
        low_ctx = lax.broadcasted_iota(jnp.int32, (n_ctx, w), 1) < 64
        for c in range(src.shape[1] // w):
            a, b = _head_halves(src[:, w * c:w * (c + 1)], low_ctx, dup)
            dst[3 * blk:, w * (2 * c):w * (2 * c + 1)] = a.astype(BF16)
            dst[3 * blk:, w * (2 * c + 1):w * (2 * c + 2)] = b.astype(BF16)

    @pl.when(j1 == 0)
    def _():
        load_ctx(kall, kctx_ref, True)

    @pl.when(j3 == 0)
    def _():
        load_ctx(vall, vctx_ref, False)

    def step(slot):
        low = lax.broadcasted_iota(jnp.int32, (nq, w), 1) < 64
        for r, src in enumerate((vp_ref, vc_ref, vn_ref)):
            vall[blk * r:blk * (r + 1), :] = src[...]
        for h in range(ATT_KV_HEADS):
            o = _dot(p_buf[1 - slot, h], vall[:, w * h:w * (h + 1)])
            for pair in range(ATT_GROUPS // 2):
                vals = []
                for g in (2 * pair, 2 * pair + 1):
                    rows = slice(nq * g, nq * (g + 1))
                    og = o[rows]
                    rg = pltpu.roll(og, 64, 1)
                    eg = e_buf[1 - slot, h, rows, :]
                    vals.append(og / (rg + eg) if g % 2 == 0 else rg / (og + eg))
                col = (ATT_GROUPS * h) // 2 + pair
                o_ref[:, w * col:w * (col + 1)] = jnp.where(low, vals[0], vals[1]).astype(BF16)
        row = lax.broadcasted_iota(jnp.int32, (blk, blk), 0)
        colm = lax.broadcasted_iota(jnp.int32, (blk, blk), 1)
        bias_prev = jnp.where(jnp.logical_and(colm >= row, j2 > 0), 0.0, NEG_INF).astype(BF16)
        bias_next = jnp.where(jnp.logical_and(colm <= row, j2 < nb - 1), 0.0, NEG_INF).astype(BF16)
        bias = jnp.concatenate([bias_prev, jnp.zeros((blk, blk), BF16), bias_next,
                                jnp.zeros((blk, n_ctx), BF16)], axis=1)
        for h in range(ATT_KV_HEADS):
            for g in range(ATT_GROUPS):
                rows = slice(nq * g, nq * (g + 1))
                sg = s_buf[1 - slot, h, rows, :].astype(BF16) + bias
                snk = jnp.full((1, 1), sink_ref[ATT_GROUPS * h + g] * LOG2E, F32)
                m = jnp.maximum(jnp.max(sg, axis=-1, keepdims=True), snk.astype(BF16))
                p_buf[slot, h, rows, :] = jnp.exp2(sg - m)
                e_buf[slot, h, rows, :] = jnp.broadcast_to(jnp.exp2(snk - m.astype(F32)), (nq, w))
        for r, src in enumerate((kp_ref, kc_ref, kn_ref)):
            kall[blk * r:blk * (r + 1), :] = src[...]
        for h in range(ATT_KV_HEADS):
            qs = []
            for g in range(ATT_GROUPS):
                a = ATT_GROUPS * h + g
                piece = q_ref[:, w * (a // 2):w * (a // 2 + 1)]
                qs.append(jnp.where(low if a % 2 == 0 else jnp.logical_not(low), piece, jnp.zeros_like(piece)))
            s_buf[slot, h] = _dot_nt(jnp.concatenate(qs, axis=0), kall[:, w * h:w * (h + 1)])

    @pl.when(i % 2 == 0)
    def _():
        step(0)

    @pl.when(i % 2 == 1)
    def _():
        step(1)


def _ret_kernel(lg_ref, q_ref, k_ref, v_ref, g_ref, *rest, t_len, c_len, prompt):
    if prompt:
        gn_ref, o_ref, sfin_ref, obuf, sf, sb = rest
    else:
        s0_ref, gn_ref, o_ref, obuf, sf, sb = rest
    nc = t_len // c_len
    h = pl.program_id(1)
    lgf = lg_ref[0, h]
    lgb = lg_ref[1, h]
    row = lax.broadcasted_iota(jnp.int32, (c_len, c_len), 0).astype(F32)
    col = lax.broadcasted_iota(jnp.int32, (c_len, c_len), 1).astype(F32)
    diff = row - col
    decay_f = jnp.where(diff >= 0, jnp.exp(lgf * jnp.maximum(diff, 0.0)), 0.0)
    decay_b = jnp.where(diff < 0, jnp.exp(lgb * jnp.maximum(-diff, 0.0)), 0.0)
    idx = lax.broadcasted_iota(jnp.int32, (c_len, 1), 0).astype(F32)
    one = jnp.ones((1, 1), F32)
    wq_f, ws_f, gc_f = jnp.exp(lgf * (idx + 1.0)), jnp.exp(lgf * (c_len - 1.0 - idx)), jnp.exp(lgf * c_len * one)
    wq_b, ws_b, gc_b = jnp.exp(lgb * (c_len - idx)), jnp.exp(lgb * idx), jnp.exp(lgb * c_len * one)
    if prompt:
        sf[...] = jnp.zeros(sf.shape, F32)
        sb[...] = jnp.zeros(sb.shape, F32)
    else:
        sf[...] = s0_ref[0]
        sb[...] = s0_ref[1]

    def rows(c):
        start = c * c_len
        return pl.ds(start if isinstance(start, int) else pl.multiple_of(start, c_len), c_len)

    def chunk(sl, state, decay, wq, ws, gc):
        qc, kc, vc = q_ref[sl, :], k_ref[sl, :], v_ref[sl, :]
        s_prev = state[...]
        sc = (_dot_nt(qc, kc) * decay).astype(BF16)
        o = _dot(sc, vc) + _dot(qc, s_prev.astype(BF16)) * wq
        kw = (kc.astype(F32) * ws).astype(BF16)
        state[...] = gc * s_prev + _dot_tn(kw, vc)
        return o

    def fwd(sl):
        return chunk(sl, sf, decay_f, wq_f, ws_f, gc_f)

    def bwd(sl):
        return chunk(sl, sb, decay_b, wq_b, ws_b, gc_b)

    def finish(sl, o):
        mu = jnp.mean(o, axis=-1, keepdims=True)
        dev = o - mu
        var = jnp.mean(dev * dev, axis=-1, keepdims=True)
        on = dev * lax.rsqrt(var + RMS_EPS) * gn_ref[...]
        o_ref[sl, :] = (g_ref[sl, :].astype(F32) * on).astype(BF16)

    half = nc // 2

    def first_half(i, carry):
        sl_f, sl_b = rows(i), rows(nc - 1 - i)
        obuf[sl_f, :] = fwd(sl_f)
        obuf[sl_b, :] = bwd(sl_b)
        return carry

    lax.fori_loop(0, half, first_half, 0)
    if nc % 2:
        sl = rows(half)
        finish(sl, fwd(sl) + bwd(sl))

    def second_half(i, carry):
        sl_f, sl_b = rows(i), rows(nc - 1 - i)
        finish(sl_f, fwd(sl_f) + obuf[sl_f, :])
        finish(sl_b, bwd(sl_b) + obuf[sl_b, :])
        return carry

    lax.fori_loop(half + nc % 2, nc, second_half, 0)
    if prompt:
        sfin_ref[0] = sf[...]
        sfin_ref[1] = sb[...]


def _softplus(x):
    return jnp.maximum(x, 0.0) + jnp.log1p(jnp.exp(-jnp.abs(x)))


def _two_steps(h, a0, u0, a1, u1):
    return a0 * h + u0, (a1 * a0) * h + (a1 * u0 + u1)


def _lru_kernel(xr_ref, prev_ref, next_ref, act_ref, cw_ref, cb_ref, wg_ref, br_ref, bi_ref, lam_ref,
                h0_ref, y_ref, hfin_ref, hf3, xc_buf, a3, u3, hb3, hc, *, nc):
    tc, d_rnn = xr_ref.shape
    bw = d_rnn // LRU_BLOCKS
    groups = tc // V7X_SUBLANES
    s = pl.program_id(1)
    is_b = s >= nc
    c = jnp.where(is_b, 2 * nc - 1 - s, s)
    base = pl.multiple_of(c * tc, tc)
    g0 = c * groups

    @pl.when(jnp.logical_not(is_b))
    def _():
        x = xr_ref[...]
        rowi = lax.broadcasted_iota(jnp.int32, (tc, d_rnn), 0)
        p1 = jnp.where(c > 0, prev_ref[V7X_SUBLANES - 1:V7X_SUBLANES, :], 0.0)
        p2 = jnp.where(c > 0, prev_ref[V7X_SUBLANES - 2:V7X_SUBLANES - 1, :], 0.0)
        n1 = jnp.where(c < nc - 1, next_ref[0:1, :], 0.0)
        xm1 = jnp.where(rowi == 0, p1, pltpu.roll(x, 1, 0))
        xm2 = jnp.where(rowi == 0, p2, jnp.where(rowi == 1, p1, pltpu.roll(x, 2, 0)))
        xp1 = jnp.where(rowi == tc - 1, n1, pltpu.roll(x, tc - 1, 0))
        xc_buf[pl.ds(base, tc), :] = (cw_ref[0:1, :] * xm2 + cw_ref[1:2, :] * xm1 + cw_ref[2:3, :] * x
                                      + cw_ref[3:4, :] * xp1 + cb_ref[...])

    xc = xc_buf[pl.ds(base, tc), :]
    xcb = xc.astype(BF16)
    c_row = (-0.5 * LRU_C) * _softplus(-lam_ref[...])
    half_xc = 0.5 * xc
    for n in range(LRU_BLOCKS):
        cs = slice(bw * n, bw * (n + 1))
        z = _dot(xcb[:, cs], wg_ref[n])
        tr = jnp.tanh(z[:, :bw] + br_ref[:, cs])
        ti = jnp.tanh(z[:, bw:] + bi_ref[:, cs])
        log_a = c_row[:, cs] * (tr + 1.0)
        a = jnp.exp(log_a)
        u = jnp.sqrt(-jnp.tanh(log_a) * (a * a + 1.0)) * ((ti + 1.0) * half_xc[:, cs])
        a3[:, :, cs] = a.reshape(groups, V7X_SUBLANES, bw)
        u3[:, :, cs] = u.reshape(groups, V7X_SUBLANES, bw)


    @pl.when(s == 0)
    def _():
        hc[...] = h0_ref[0:1, :]

    @pl.when(s == nc)
    def _():
        hc[...] = h0_ref[1:2, :]

    @pl.when(jnp.logical_not(is_b))
    def _():
        def group(i, h):
            for r in range(0, V7X_SUBLANES, 2):
                h0, h = _two_steps(h, a3[i, pl.ds(r, 1), :], u3[i, pl.ds(r, 1), :],
                                   a3[i, pl.ds(r + 1, 1), :], u3[i, pl.ds(r + 1, 1), :])
                hf3[g0 + i, pl.ds(r, 1), :] = h0
                hf3[g0 + i, pl.ds(r + 1, 1), :] = h
            return h
        h = lax.fori_loop(0, groups, group, hc[...])
        hc[...] = h

        @pl.when(s == nc - 1)
        def _():
            hfin_ref[0:1, :] = h

    @pl.when(is_b)
    def _():
        def group(k, h):
            i = groups - 1 - k
            for r in range(V7X_SUBLANES - 1, 0, -2):
                h0, h = _two_steps(h, a3[i, pl.ds(r, 1), :], u3[i, pl.ds(r, 1), :],
                                   a3[i, pl.ds(r - 1, 1), :], u3[i, pl.ds(r - 1, 1), :])
                hb3[i, pl.ds(r, 1), :] = h0
                hb3[i, pl.ds(r - 1, 1), :] = h
            return h
        h = lax.fori_loop(0, groups, group, hc[...])
        hc[...] = h
        rec = (hf3[pl.ds(g0, groups)] + hb3[...]).reshape(tc, d_rnn)
        y_ref[...] = (act_ref[...] * rec).astype(BF16)

        @pl.when(s == 2 * nc - 1)
        def _():
            hfin_ref[1:2, :] = h


def _out_mlp_kernel(*refs, n_prompt, split_x, split_out):
    x, rest = _read_x(refs, n_prompt, split_x)
    yp_ref, ys_ref, mod_ref, gm_ref, wo_ref, wu_ref, wd_ref = rest[:7]
    d = x.shape[1]
    d_ff = wu_ref.shape[1]
    is_prompt = pl.program_id(0) < n_prompt
    y = jnp.where(is_prompt, yp_ref[...], ys_ref[...])
    x1 = x + mod_ref[:, 2 * d:3 * d] * _dot(y, wo_ref[...])
    hn = _modulate(x1, gm_ref[...], mod_ref[:, 3 * d:4 * d], mod_ref[:, 4 * d:5 * d]).astype(BF16)
    acc = jnp.zeros(x1.shape, F32)
    for c in range(d_ff // FF_CHUNK):
        hh = jnp.maximum(_dot(hn, wu_ref[:, FF_CHUNK * c:FF_CHUNK * (c + 1)]), 0.0)
        acc = acc + _dot((hh * hh).astype(BF16), wd_ref[FF_CHUNK * c:FF_CHUNK * (c + 1), :])
    out = x1 + mod_ref[:, 5 * d:6 * d] * acc
    if split_out:
        op_ref, os_ref = rest[7:]

        @pl.when(is_prompt)
        def _():
            op_ref[...] = out

        @pl.when(jnp.logical_not(is_prompt))
        def _():
            os_ref[...] = out
    else:
        rest[7][...] = out


class _Dims:
    def __init__(self, bp, lp, bs, ts, d):
        assert bp * lp == ts, "prompt tokens must fill exactly one segment"
        self.bp, self.lp, self.bs, self.ts, self.d = bp, lp, bs, ts, d
        self.nseg = 1 + bs
        self.n = self.nseg * ts
        self.tm = min(TOKEN_TILE, ts)
        assert ts % self.tm == 0 and ts % ATT_BLOCK == 0
        assert ts % RET_KERNEL_CHUNK == 0 and lp % min(RET_KERNEL_CHUNK, lp) == 0
        self.tps = ts // self.tm
        self.ntiles = self.n // self.tm


def _in_kernel(body, dm, split_x):
    return functools.partial(body, n_prompt=dm.tps, split_x=split_x)


def _attn_layer(dm, layer, slot, xs, norm_mix, mods, w_in, q_gain, k_gain, sink, cache_k, cache_v, rope):
    d, tm, tps, n = dm.d, dm.tm, dm.tps, dm.n
    split_x = len(xs) == 2
    hd = ATT_HEAD_DIM
    q_w, kv_w = ATT_HEADS * hd, ATT_KV_HEADS * hd
    gain = jnp.concatenate([jnp.tile(q_gain, ATT_HEADS), jnp.tile(k_gain, ATT_KV_HEADS)])[None, :]
    gw = q_w + kv_w
    assert (q_w + kv_w) % gw == 0 and gw % V7X_LANES == 0 and gw // hd <= V7X_LANES
    head_of_lane = np.arange(gw) // hd
    hsum = jnp.asarray(head_of_lane[:, None] == np.arange(V7X_LANES)[None, :], BF16)
    hexp = jnp.asarray(np.arange(V7X_LANES)[:, None] == head_of_lane[None, :], BF16)
    cos, sin = rope
    nt = dm.ntiles
    cur = lambda i: jnp.minimum(i, nt - 1)
    prev = lambda i: jnp.maximum(i - 1, 0)
    tok = lambda w: pl.BlockSpec((tm, w), lambda i: (prev(i), 0))
    q, k, v, kf, vf = pl.pallas_call(
        _in_kernel(_attn_in_kernel, dm, split_x),
        grid=(nt + 1,),
        in_specs=_token_specs(tm, tps, d, layer, split_x, cur) + [
            _resident(w_in.shape), _resident(gain.shape)] + _rope_specs(tm, tps, V7X_LANES, prev) + [
            _resident(hsum.shape), _resident(hexp.shape)],
        out_specs=[tok(q_w), tok(2 * kv_w), tok(2 * kv_w), tok(kv_w), tok(kv_w)],
        out_shape=[jax.ShapeDtypeStruct((n, q_w), BF16), jax.ShapeDtypeStruct((n, 2 * kv_w), BF16),
                   jax.ShapeDtypeStruct((n, 2 * kv_w), BF16), jax.ShapeDtypeStruct((n, kv_w), F32),
                   jax.ShapeDtypeStruct((n, kv_w), F32)],
        scratch_shapes=[pltpu.VMEM((2, tm, q_w + 2 * kv_w), F32)],
        compiler_params=_cparams(("arbitrary",)),
        name="attn_in",
    )(*xs, norm_mix, mods, w_in, gain, cos, sin, hsum, hexp)
    kf, vf = kf[:dm.ts], vf[:dm.ts]

    smem = pl.BlockSpec(memory_space=pltpu.SMEM)
    lp, bp, bs, ts = dm.lp, dm.bp, dm.bs, dm.ts
    y_p = pl.pallas_call(
        _attn_prompt_kernel,
        grid=(bp,),
        in_specs=[smem, pl.BlockSpec((lp, q_w), lambda b: (b, 0)),
                  pl.BlockSpec((lp, 2 * kv_w), lambda b: (b, 0)),
                  pl.BlockSpec((lp, 2 * kv_w), lambda b: (b, 0))],
        out_specs=pl.BlockSpec((lp, q_w), lambda b: (b, 0)),
        out_shape=jax.ShapeDtypeStruct((ts, q_w), BF16),
        compiler_params=_cparams(("arbitrary",)),
        name="attn_prompt",
    )(sink, q, k, v)

    blk = ATT_BLOCK
    nb = ts // blk
    nblocks = bs * nb
    past = cache_k.shape[2]
    nk = 3 * blk + past
    c1 = lambda i: jnp.minimum(i, nblocks - 1)
    c3 = lambda i: jnp.clip(i - 2, 0, nblocks - 1)
    cur = lambda c: lambda i: (nb + c(i), 0)
    prv = lambda c: lambda i: (nb + c(i) - jnp.where(c(i) % nb > 0, 1, 0), 0)
    nxt = lambda c: lambda i: (nb + c(i) + jnp.where(c(i) % nb < nb - 1, 1, 0), 0)
    kv_spec = lambda f: pl.BlockSpec((blk, 2 * kv_w), f)
    ctx_spec = lambda c: pl.BlockSpec((None, None, past, kv_w), lambda i: (c(i) // nb, slot, 0, 0))
    rows = ATT_GROUPS * blk
    y_s = pl.pallas_call(
        functools.partial(_attn_sample_kernel, nb=nb, nblocks=nblocks),
        grid=(nblocks + 2,),
        in_specs=[smem, pl.BlockSpec((blk, q_w), cur(c1)),
                  kv_spec(prv(c1)), kv_spec(cur(c1)), kv_spec(nxt(c1)),
                  kv_spec(prv(c3)), kv_spec(cur(c3)), kv_spec(nxt(c3)), ctx_spec(c1), ctx_spec(c3)],
        out_specs=pl.BlockSpec((blk, q_w), lambda i: (c3(i), 0)),
        out_shape=jax.ShapeDtypeStruct((bs * ts, q_w), BF16),
        scratch_shapes=[pltpu.VMEM((nk, 2 * kv_w), BF16), pltpu.VMEM((nk, 2 * kv_w), BF16),
                        pltpu.VMEM((2, ATT_KV_HEADS, rows, nk), F32),
                        pltpu.VMEM((2, ATT_KV_HEADS, rows, nk), BF16),
                        pltpu.VMEM((2, ATT_KV_HEADS, rows, V7X_LANES), F32)],
        compiler_params=_cparams(("arbitrary",)),
        name="attn_sample",
    )(sink, q, k, k, k, v, v, v, cache_k, cache_v)
    return y_p, y_s, kf, vf


def _ret_layer(dm, layer, slot, xs, norm_mix, mods, w_in, gn_gain, log_decay, state, rope):
    d, tm, tps, n = dm.d, dm.tm, dm.tps, dm.n
    split_x = len(xs) == 2
    e1, ev = RET_HEADS * RET_DK, RET_HEADS * RET_DV
    cos, sin = rope
    tok = lambda w: pl.BlockSpec((tm, w), lambda i: (i, 0))
    q, k, v, g = pl.pallas_call(
        _in_kernel(_ret_in_kernel, dm, split_x),
        grid=(dm.ntiles,),
        in_specs=(_token_specs(tm, tps, d, layer, split_x) + [_resident(w_in.shape)]
                  + _rope_specs(tm, tps, RET_DK)),
        out_specs=[tok(e1), tok(e1), tok(ev), tok(ev)],
        out_shape=[jax.ShapeDtypeStruct((n, e1), BF16), jax.ShapeDtypeStruct((n, e1), BF16),
                   jax.ShapeDtypeStruct((n, ev), BF16), jax.ShapeDtypeStruct((n, ev), BF16)],
        compiler_params=_cparams(("arbitrary",)),
        name="ret_in",
    )(*xs, norm_mix, mods, w_in, cos, sin)

    smem = pl.BlockSpec(memory_space=pltpu.SMEM)
    gn = gn_gain[None, :]
    lp, bp, bs, ts = dm.lp, dm.bp, dm.bs, dm.ts
    state_spec = lambda f: pl.BlockSpec((None, None, 2, None, RET_DK, RET_DV), f)

    def call(t_len, nbatch, seg0, prompt, name):
        rows = lambda w: pl.BlockSpec((t_len, w), lambda b, h: (seg0 + b, h))
        in_specs = [smem, rows(RET_DK), rows(RET_DK), rows(RET_DV), rows(RET_DV)]
        args = [log_decay, q, k, v, g]
        out_shape = [jax.ShapeDtypeStruct((nbatch * t_len, ev), BF16)]
        out_specs = [pl.BlockSpec((t_len, RET_DV), lambda b, h: (b, h))]
        if prompt:
            out_shape.append(jax.ShapeDtypeStruct((nbatch, 1, 2, RET_HEADS, RET_DK, RET_DV), F32))
            out_specs.append(state_spec(lambda b, h: (b, 0, 0, h, 0, 0)))
        else:
            in_specs.append(state_spec(lambda b, h: (b, slot, 0, h, 0, 0)))
            args.append(state)
        in_specs.append(pl.BlockSpec((1, RET_DV), lambda b, h: (0, h)))
        args.append(gn)
        return pl.pallas_call(
            functools.partial(_ret_kernel, t_len=t_len, c_len=min(RET_KERNEL_CHUNK, t_len), prompt=prompt),
            grid=(nbatch, RET_HEADS),
            in_specs=in_specs, out_specs=out_specs, out_shape=out_shape,
            scratch_shapes=[pltpu.VMEM((t_len, RET_DV), F32), pltpu.VMEM((RET_DK, RET_DV), F32),
                            pltpu.VMEM((RET_DK, RET_DV), F32)],
            compiler_params=_cparams(("arbitrary", "arbitrary")),
            name=name,
        )(*args)

    y_p, s_new = call(lp, bp, 0, True, "ret_prompt")
    (y_s,) = call(ts, bs, 1, False, "ret_sample")
    return y_p, y_s, s_new


def _lru_layer(dm, layer, slot, xs, norm_mix, mods, w_in, conv_w, conv_b, w_r, b_r, w_i, b_i, lam, state):
    d, tm, tps, n = dm.d, dm.tm, dm.tps, dm.n
    split_x = len(xs) == 2
    d_rnn = w_in.shape[1] // 2
    tok = lambda w: pl.BlockSpec((tm, w), lambda i: (i, 0))
    act, xr = pl.pallas_call(
        _in_kernel(_lru_in_kernel, dm, split_x),
        grid=(dm.ntiles,),
        in_specs=_token_specs(tm, tps, d, layer, split_x) + [_resident(w_in.shape)],
        out_specs=[tok(d_rnn), tok(d_rnn)],
        out_shape=[jax.ShapeDtypeStruct((n, d_rnn), F32), jax.ShapeDtypeStruct((n, d_rnn), F32)],
        compiler_params=_cparams(("arbitrary",)),
        name="lru_in",
    )(*xs, norm_mix, mods, w_in)

    wg = (0.5 * jnp.concatenate([w_r, w_i], axis=-1)).astype(BF16)
    b_r, b_i = 0.5 * b_r, 0.5 * b_i
    vec = lambda a: a[:, None, :]
    lp, bp, bs, ts = dm.lp, dm.bp, dm.bs, dm.ts
    sub = V7X_SUBLANES

    def call(t_len, nbatch, row0, h0, h0_slot, name):
        tc = t_len if t_len <= 256 else min(512, t_len // 2)
        nc = t_len // tc
        cb = row0 // tc
        early = lambda s: jnp.minimum(s, nc - 1)
        late = lambda s: jnp.where(s >= nc, 2 * nc - 1 - s, nc - 1)
        dirv = lambda s: jnp.where(s >= nc, 1, 0)
        r8 = tc // sub
        first8 = lambda b: (row0 + b * t_len) // sub
        prev8 = lambda b, s: (jnp.maximum(first8(b) + early(s) * r8 - 1, first8(b)), 0)
        next8 = lambda b, s: (jnp.minimum(first8(b) + (early(s) + 1) * r8, first8(b) + t_len // sub - 1), 0)
        dspec = lambda shape: pl.BlockSpec((None,) + shape, lambda b, s: (dirv(s),) + (0,) * len(shape))
        return pl.pallas_call(
            functools.partial(_lru_kernel, nc=nc),
            grid=(nbatch, 2 * nc),
            in_specs=[pl.BlockSpec((tc, d_rnn), lambda b, s: (cb + b * nc + early(s), 0)),
                      pl.BlockSpec((sub, d_rnn), prev8), pl.BlockSpec((sub, d_rnn), next8),
                      pl.BlockSpec((tc, d_rnn), lambda b, s: (cb + b * nc + late(s), 0)),
                      pl.BlockSpec((CONV_W, d_rnn), lambda b, s: (0, 0)),
                      pl.BlockSpec((1, d_rnn), lambda b, s: (0, 0)),
                      dspec(wg.shape[1:]), dspec((1, d_rnn)), dspec((1, d_rnn)), dspec((1, d_rnn)),
                      pl.BlockSpec((None, None, 2, d_rnn), lambda b, s: (b, h0_slot, 0, 0))],
            out_specs=[pl.BlockSpec((tc, d_rnn), lambda b, s: (b * nc + late(s), 0)),
                       pl.BlockSpec((None, 2, d_rnn), lambda b, s: (b, 0, 0))],
            out_shape=[jax.ShapeDtypeStruct((nbatch * t_len, d_rnn), BF16),
                       jax.ShapeDtypeStruct((nbatch, 2, d_rnn), F32)],
            scratch_shapes=[pltpu.VMEM((t_len // sub, sub, d_rnn), F32), pltpu.VMEM((t_len, d_rnn), F32),
                            pltpu.VMEM((tc // sub, sub, d_rnn), F32), pltpu.VMEM((tc // sub, sub, d_rnn), F32),
                            pltpu.VMEM((tc // sub, sub, d_rnn), F32), pltpu.VMEM((1, d_rnn), F32)],
            compiler_params=_cparams(("arbitrary", "arbitrary")),
            name=name,
        )(xr, xr, xr, act, conv_w, conv_b[None, :], wg, vec(b_r), vec(b_i), vec(lam), h0)

    y_p, h_new = call(lp, bp, 0, jnp.zeros((bp, 1, 2, d_rnn), F32), 0, "lru_prompt")
    y_s, _ = call(ts, bs, ts, state, slot, "lru_sample")
    return y_p, y_s, h_new


def _out_mlp(dm, layer, xs, y_p, y_s, norm_mlp, mods, w_out, w_up, w_down, split_out):
    d, tm, tps, n = dm.d, dm.tm, dm.tps, dm.n
    split_x = len(xs) == 2
    din = w_out.shape[0]
    first = lambda i: (jnp.minimum(i, tps - 1), 0)
    second = lambda i: (jnp.maximum(i - tps, 0), 0)
    if split_out:
        out_specs = [pl.BlockSpec((tm, d), first), pl.BlockSpec((tm, d), second)]
        out_shape = [jax.ShapeDtypeStruct((dm.ts, d), F32), jax.ShapeDtypeStruct((n - dm.ts, d), F32)]
    else:
        out_specs = pl.BlockSpec((tm, d), lambda i: (i, 0))
        out_shape = jax.ShapeDtypeStruct((n, d), F32)
    return pl.pallas_call(
        functools.partial(_out_mlp_kernel, n_prompt=tps, split_x=split_x, split_out=split_out),
        grid=(dm.ntiles,),
        in_specs=_x_specs(tm, tps, d, split_x) + [
            pl.BlockSpec((tm, din), first), pl.BlockSpec((tm, din), second),
            pl.BlockSpec((None, None, 1, N_MOD * d), lambda i: (layer, i // tps, 0, 0)),
            pl.BlockSpec((None, 1, d), lambda i: (layer, 0, 0)),
            _resident(w_out.shape), _layer_resident(w_up.shape, layer), _layer_resident(w_down.shape, layer)],
        out_specs=out_specs, out_shape=out_shape,
        compiler_params=_cparams(("arbitrary",)),
        name="out_mlp",
    )(*xs, y_p, y_s, mods, norm_mlp, w_out, w_up, w_down)


def kernel(x_prompt, x_sample, cache_attn_k, cache_attn_v, state_ret, state_lru, c, c_ctx, norm_mix, norm_mlp, w_ada, b_ada, w_up, w_down, attn_w_in, attn_w_out, attn_q_gain, attn_k_gain, attn_sink, ret_w_in, ret_w_out, ret_gn_gain, ret_log_decay, lru_w_in, lru_conv_w, lru_conv_b, lru_w_r, lru_b_r, lru_w_i, lru_b_i, lru_lambda, lru_w_out):
    bp, lp, d = x_prompt.shape
    bs, ts, _ = x_sample.shape
    dm = _Dims(bp, lp, bs, ts, d)
    depth = w_ada.shape[0]
    assert dm.nseg <= MOD_ROWS

    xs = (x_prompt.reshape(ts, d), x_sample.reshape(bs * ts, d))
    cvec = jnp.concatenate([c_ctx[None, :], c, jnp.zeros((MOD_ROWS - dm.nseg, d), F32)], axis=0)
    mods = _modulation(cvec, w_ada, b_ada)[:, :dm.nseg].reshape(depth, dm.nseg, 1, N_MOD * d)
    nmix = norm_mix[:, None, :]
    nmlp = norm_mlp[:, None, :]

    rope_attn = tuple(jnp.tile(t, (1, 1, 2)) for t in _rope_tables(ts, ATT_HEAD_DIM // 2))
    rope_ret = _rope_tables(ts, RET_DK // 2)
    kv_w = ATT_KV_HEADS * ATT_HEAD_DIM
    past = cache_attn_k.shape[2]
    ck = cache_attn_k.reshape(bs, -1, past, kv_w)
    cv = cache_attn_v.reshape(bs, -1, past, kv_w)

    w_up_b, w_down_b = w_up.astype(BF16), w_down.astype(BF16)

    new_k, new_v, new_ret, new_lru = [], [], [], []
    for layer in range(depth):
        kind, slot = layer % 3, layer // 3
        if kind == 0:
            y_p, y_s, kf, vf = _attn_layer(dm, layer, slot, xs, nmix, mods, attn_w_in[slot].astype(BF16),
                                           attn_q_gain[slot], attn_k_gain[slot], attn_sink[slot],
                                           ck, cv, rope_attn)
            new_k.append(kf.reshape(bp, lp, ATT_KV_HEADS, ATT_HEAD_DIM))
            new_v.append(vf.reshape(bp, lp, ATT_KV_HEADS, ATT_HEAD_DIM))
            w_out = attn_w_out[slot]
        elif kind == 1:
            y_p, y_s, s_new = _ret_layer(dm, layer, slot, xs, nmix, mods, ret_w_in[slot].astype(BF16),
                                         ret_gn_gain[slot], ret_log_decay[slot], state_ret, rope_ret)
            new_ret.append(s_new)
            w_out = ret_w_out[slot]
        else:
            y_p, y_s, h_new = _lru_layer(dm, layer, slot, xs, nmix, mods, lru_w_in[slot].astype(BF16),
                                         lru_conv_w[slot], lru_conv_b[slot], lru_w_r[slot], lru_b_r[slot],
                                         lru_w_i[slot], lru_b_i[slot], lru_lambda[slot], state_lru)
            new_lru.append(h_new)
            w_out = lru_w_out[slot]
        last = layer == depth - 1
        out = _out_mlp(dm, layer, xs, y_p, y_s, nmlp, mods, w_out.astype(BF16), w_up_b, w_down_b,
                       split_out=last)
        xs = tuple(out) if last else (out,)

    y_prompt = xs[0].reshape(bp, lp, d)
    y_sample = xs[1].reshape(bs, ts, d)
    return (y_prompt, y_sample, jnp.stack(new_k, axis=1), jnp.stack(new_v, axis=1),
            jnp.concatenate(new_ret, axis=1), jnp.stack(new_lru, axis=1))
```

```python
import functools

import jax
import jax.numpy as jnp
import numpy as np
from jax import lax
from jax.experimental import pallas as pl
from jax.experimental.pallas import tpu as pltpu

F32 = jnp.float32
BF16 = jnp.bfloat16

N_MOD = 6
RMS_EPS = 1e-6
ROPE_BASE = 10000.0
NEG_INF = -1e30
LOG2E = 1.4426950408889634
GRID_W = 64
ATT_HEADS = 16
ATT_KV_HEADS = 4
ATT_HEAD_DIM = 64
ATT_GROUPS = ATT_HEADS // ATT_KV_HEADS
ATT_BLOCK = 128
RET_HEADS = 4
RET_DK = 256
RET_DV = 512
LRU_BLOCKS = 8
CONV_W = 4
LRU_C = 8.0

V7X_LANES = 128
V7X_SUBLANES = 8
V7X_VMEM_BYTES = 64 * 1024 * 1024
VMEM_LIMIT = V7X_VMEM_BYTES - 8 * 1024 * 1024

TOKEN_TILE = 512
MOD_ROWS = 16
FF_CHUNK = 1024
RET_KERNEL_CHUNK = 256


def _cparams(sem):
    return pltpu.CompilerParams(dimension_semantics=sem, vmem_limit_bytes=VMEM_LIMIT)


def _resident(shape):
    zeros = (0,) * len(shape)
    return pl.BlockSpec(shape, lambda *_: zeros, pipeline_mode=pl.Buffered(1))


def _layer_resident(shape, layer):
    zeros = (0,) * (len(shape) - 1)
    return pl.BlockSpec((None,) + tuple(shape[1:]), lambda *_: (layer,) + zeros, pipeline_mode=pl.Buffered(1))


def _dot(a, b):
    return jnp.dot(a, b, preferred_element_type=F32)


def _dot_nt(a, b):
    return lax.dot_general(a, b, (((1,), (1,)), ((), ())), preferred_element_type=F32)


def _dot_tn(a, b):
    return lax.dot_general(a, b, (((0,), (0,)), ((), ())), preferred_element_type=F32)


def _silu(x):
    return x * jax.nn.sigmoid(x)


def _modulate(x, g, shift, scale):
    ms = jnp.mean(x * x, axis=-1, keepdims=True)
    y = x * lax.rsqrt(ms + RMS_EPS) * g
    return y * (1.0 + scale) + shift


def _split_bf16(x):
    hi = x.astype(BF16)
    lo = (x - hi.astype(F32)).astype(BF16)
    return hi, lo


def _read_x(refs, n_prompt, split_x):
    if split_x:
        x = jnp.where(pl.program_id(0) < n_prompt, refs[0][...], refs[1][...])
        return x, refs[2:]
    return refs[0][...], refs[1:]


def _mod_kernel(c_ref, w_ref, b_ref, o_ref):
    s = _silu(c_ref[...]).astype(BF16)
    o_ref[...] = _dot(s, w_ref[...].astype(BF16)) + b_ref[...]


def _modulation(cvec, w_ada, b_ada):
    depth, d, w = w_ada.shape
    tn = 1536
    return pl.pallas_call(
        _mod_kernel,
        grid=(depth, w // tn),
        in_specs=[pl.BlockSpec((MOD_ROWS, d), lambda l, j: (0, 0)),
                  pl.BlockSpec((None, d, tn), lambda l, j: (l, 0, j)),
                  pl.BlockSpec((None, 1, tn), lambda l, j: (l, 0, j))],
        out_specs=pl.BlockSpec((None, MOD_ROWS, tn), lambda l, j: (l, 0, j)),
        out_shape=jax.ShapeDtypeStruct((depth, MOD_ROWS, w), F32),
        compiler_params=_cparams(("arbitrary", "arbitrary")),
        name="modulation",
    )(cvec, w_ada, b_ada.reshape(depth, 1, w))


def _rope_tables(t_len, sect):
    half = sect // 2
    inv = ROPE_BASE ** (-jnp.arange(half, dtype=F32) / half)
    t = jnp.arange(t_len, dtype=jnp.int32)

    def one(pos):
        ang = pos.astype(F32)[:, None] * inv[None, :]
        c, s = jnp.cos(ang), jnp.sin(ang)
        return jnp.concatenate([c, c], axis=1), jnp.concatenate([-s, s], axis=1)

    cr, sr = one(t // GRID_W)
    cc, sc = one(t % GRID_W)
    cos = jnp.concatenate([cr, cc], axis=1)
    sin = jnp.concatenate([sr, sc], axis=1)
    return (jnp.stack([jnp.ones_like(cos), cos]), jnp.stack([jnp.zeros_like(sin), sin]))


def _dup_halves(x, lane_low):
    r = pltpu.roll(x, 64, 1)
    return jnp.where(lane_low, x, r), jnp.where(lane_low, r, x)


def _attn_in_kernel(*refs, n_prompt, split_x):
    x, rest = _read_x(refs, n_prompt, split_x)
    (g_ref, mod_ref, w_ref, gain_ref, cos_ref, sin_ref, hsum_ref, hexp_ref,
     q_out, k_out, v_out, kf_out, vf_out, proj_s) = rest
    tm, d = x.shape
    i = pl.program_id(0)
    qk_w = hsum_ref.shape[0]
    n_q = ATT_HEADS * ATT_HEAD_DIM // V7X_LANES

    @pl.when(i == 0)
    def _():
        proj_s[1] = jnp.zeros(proj_s.shape[1:], F32)

    def epilogue(proj):
        lane = lax.broadcasted_iota(jnp.int32, (tm, V7X_LANES), 1)
        first = (lane % 32) < 16
        low = lane < 64
        cos = cos_ref[...]
        sin = sin_ref[...]
        qk = proj[:, :qk_w]
        ssum = _dot((qk * qk).astype(BF16), hsum_ref[...])
        r_hi, r_lo = _split_bf16(lax.rsqrt(ssum * (1.0 / ATT_HEAD_DIM) + RMS_EPS))
        qkn = qk * (_dot(r_hi, hexp_ref[...]) + _dot(r_lo, hexp_ref[...])) * gain_ref[...]
        for j in range(qk_w // V7X_LANES):
            xj = qkn[:, V7X_LANES * j:V7X_LANES * (j + 1)]
            rot = jnp.where(first, pltpu.roll(xj, V7X_LANES - 16, 1), pltpu.roll(xj, 16, 1))
            yj = xj * cos + rot * sin
            if j < n_q:
                q_out[:, V7X_LANES * j:V7X_LANES * (j + 1)] = (yj * (ATT_HEAD_DIM ** -0.5 * LOG2E)).astype(BF16)
            else:
                jj = j - n_q
                kf_out[:, V7X_LANES * jj:V7X_LANES * (jj + 1)] = yj
                a, b = _dup_halves(yj, low)
                k_out[:, V7X_LANES * (2 * jj):V7X_LANES * (2 * jj + 1)] = a.astype(BF16)
                k_out[:, V7X_LANES * (2 * jj + 1):V7X_LANES * (2 * jj + 2)] = b.astype(BF16)
        v = proj[:, qk_w:]
        vf_out[...] = v
        for jj in range(v.shape[1] // V7X_LANES):
            a, b = _dup_halves(v[:, V7X_LANES * jj:V7X_LANES * (jj + 1)], low)
            v_out[:, V7X_LANES * (2 * jj):V7X_LANES * (2 * jj + 1)] = a.astype(BF16)
            v_out[:, V7X_LANES * (2 * jj + 1):V7X_LANES * (2 * jj + 2)] = b.astype(BF16)

    def step(slot):
        epilogue(proj_s[1 - slot])
        h = _modulate(x, g_ref[...], mod_ref[:, 0:d], mod_ref[:, d:2 * d]).astype(BF16)
        proj_s[slot] = _dot(h, w_ref[...])

    @pl.when(i % 2 == 0)
    def _():
        step(0)

    @pl.when(i % 2 == 1)
    def _():
        step(1)


def _ret_in_kernel(*refs, n_prompt, split_x):
    x, (g_ref, mod_ref, w_ref, cos_ref, sin_ref, q_out, k_out, v_out, g_out) = _read_x(refs, n_prompt, split_x)
    d = x.shape[1]
    e1 = RET_HEADS * RET_DK
    e3 = 2 * e1 + RET_HEADS * RET_DV
    h = _modulate(x, g_ref[...], mod_ref[:, 0:d], mod_ref[:, d:2 * d]).astype(BF16)
    n_chunks = e1 // V7X_LANES
    for sec, out in ((0, q_out), (1, k_out)):
        p = _dot(h, w_ref[:, sec * e1:(sec + 1) * e1])
        for j in range(n_chunks):
            s = j % 2
            xj = p[:, V7X_LANES * j:V7X_LANES * (j + 1)]
            yj = (xj * cos_ref[:, V7X_LANES * s:V7X_LANES * (s + 1)]
                  + pltpu.roll(xj, 64, 1) * sin_ref[:, V7X_LANES * s:V7X_LANES * (s + 1)])
            if sec == 1:
                yj = yj * (RET_DK ** -0.5)
            out[:, V7X_LANES * j:V7X_LANES * (j + 1)] = yj.astype(BF16)
    v_out[...] = _dot(h, w_ref[:, 2 * e1:e3]).astype(BF16)
    g_out[...] = _silu(_dot(h, w_ref[:, e3:])).astype(BF16)


def _lru_in_kernel(*refs, n_prompt, split_x):
    x, (g_ref, mod_ref, w_ref, act_out, xr_out) = _read_x(refs, n_prompt, split_x)
    d = x.shape[1]
    d_rnn = act_out.shape[1]
    h = _modulate(x, g_ref[...], mod_ref[:, 0:d], mod_ref[:, d:2 * d]).astype(BF16)
    act_out[...] = jax.nn.gelu(_dot(h, w_ref[:, :d_rnn]), approximate=True)
    xr_out[...] = _dot(h, w_ref[:, d_rnn:])


def _x_specs(tm, tps, d, split_x, tile=lambda i: i):
    if split_x:
        return [pl.BlockSpec((tm, d), lambda i: (jnp.minimum(tile(i), tps - 1), 0)),
                pl.BlockSpec((tm, d), lambda i: (jnp.maximum(tile(i) - tps, 0), 0))]
    return [pl.BlockSpec((tm, d), lambda i: (tile(i), 0))]


def _token_specs(tm, tps, d, layer, split_x, tile=lambda i: i):
    return _x_specs(tm, tps, d, split_x, tile) + [
        pl.BlockSpec((None, 1, d), lambda i: (layer, 0, 0)),
        pl.BlockSpec((None, None, 1, N_MOD * d), lambda i: (layer, tile(i) // tps, 0, 0))]


def _rope_specs(tm, tps, width, tile=lambda i: i):
    spec = pl.BlockSpec((None, tm, width), lambda i: (jnp.minimum(tile(i) // tps, 1), tile(i) % tps, 0))
    return [spec, spec]


def _attend(q_ref, k_ref, v_ref, bias, sink_ref, o_ref):
    nq = q_ref.shape[0]
    w = V7X_LANES
    low = lax.broadcasted_iota(jnp.int32, (nq, w), 1) < 64
    for h in range(ATT_KV_HEADS):
        hcol = slice(w * h, w * (h + 1))
        qs = []
        for g in range(ATT_GROUPS):
            a = ATT_GROUPS * h + g
            piece = q_ref[:, w * (a // 2):w * (a // 2 + 1)]
            qs.append(jnp.where(low if a % 2 == 0 else jnp.logical_not(low), piece, jnp.zeros_like(piece)))
        s = _dot_nt(jnp.concatenate(qs, axis=0), k_ref[:, hcol])
        ps, dens = [], []
        for g in range(ATT_GROUPS):
            sg = s[nq * g:nq * (g + 1)]
            if bias is not None:
                sg = sg + bias
            snk = sink_ref[ATT_GROUPS * h + g] * LOG2E
            m = jnp.maximum(jnp.max(sg, axis=-1, keepdims=True), snk)
            p = jnp.exp2(sg - m)
            dens.append(jnp.sum(p, axis=-1, keepdims=True) + jnp.exp2(snk - m))
            ps.append(p.astype(BF16))
        o = _dot(jnp.concatenate(ps, axis=0), v_ref[:, hcol])
        for pair in range(ATT_GROUPS // 2):
            g0, g1 = 2 * pair, 2 * pair + 1
            o0 = o[nq * g0:nq * (g0 + 1)] / dens[g0]
            o1 = o[nq * g1:nq * (g1 + 1)] / dens[g1]
            col = (ATT_GROUPS * h) // 2 + pair
            o_ref[:, w * col:w * (col + 1)] = jnp.where(low, o0, o1).astype(BF16)


def _attn_prompt_kernel(sink_ref, q_ref, k_ref, v_ref, o_ref):
    _attend(q_ref, k_ref, v_ref, None, sink_ref, o_ref)


def _attn_sample_kernel(sink_ref, q_ref, kp_ref, kc_ref, kn_ref, vp_ref, vc_ref, vn_ref,
                        kctx_ref, vctx_ref, o_ref, kall, vall, s_buf, p_buf, den_buf, *, nb, nblocks):
    i = pl.program_id(0)
    blk, w = ATT_BLOCK, V7X_LANES
    nq = blk
    n_ctx = kctx_ref.shape[0]
    j1 = jnp.minimum(i, nblocks - 1) % nb
    j2 = jnp.clip(i - 1, 0, nblocks - 1) % nb
    j3 = jnp.clip(i - 2, 0, nblocks - 1) % nb

    @pl.when(i == 0)
    def _():
        s_buf[...] = jnp.zeros(s_buf.shape, BF16)
        p_buf[...] = jnp.zeros(p_buf.shape, BF16)
        den_buf[...] = jnp.ones(den_buf.shape, F32)

    def load_ctx(dst, src):
        low_ctx = lax.broadcasted_iota(jnp.int32, (n_ctx, w), 1) < 64
        for c in range(src.shape[1] // w):
            a, b = _dup_halves(src[:, w * c:w * (c + 1)], low_ctx)
            dst[3 * blk:, w * (2 * c):w * (2 * c + 1)] = a.astype(BF16)
            dst[3 * blk:, w * (2 * c + 1):w * (2 * c + 2)] = b.astype(BF16)

    @pl.when(j1 == 0)
    def _():
        load_ctx(kall, kctx_ref)

    @pl.when(j3 == 0)
    def _():
        load_ctx(vall, vctx_ref)

    def step(slot):
        low = lax.broadcasted_iota(jnp.int32, (nq, w), 1) < 64
        for r, src in enumerate((vp_ref, vc_ref, vn_ref)):
            vall[blk * r:blk * (r + 1), :] = src[...]
        for h in range(ATT_KV_HEADS):
            o = _dot(p_buf[1 - slot, h], vall[:, w * h:w * (h + 1)])
            for pair in range(ATT_GROUPS // 2):
                g0, g1 = 2 * pair, 2 * pair + 1
                o0 = o[nq * g0:nq * (g0 + 1)] / den_buf[1 - slot, h, nq * g0:nq * (g0 + 1), :]
                o1 = o[nq * g1:nq * (g1 + 1)] / den_buf[1 - slot, h, nq * g1:nq * (g1 + 1), :]
                col = (ATT_GROUPS * h) // 2 + pair
                o_ref[:, w * col:w * (col + 1)] = jnp.where(low, o0, o1).astype(BF16)
        row = lax.broadcasted_iota(jnp.int32, (blk, blk), 0)
        colm = lax.broadcasted_iota(jnp.int32, (blk, blk), 1)
        bias_prev = jnp.where(jnp.logical_and(colm >= row, j2 > 0), 0.0, NEG_INF).astype(BF16)
        bias_next = jnp.where(jnp.logical_and(colm <= row, j2 < nb - 1), 0.0, NEG_INF).astype(BF16)
        bias = jnp.concatenate([bias_prev, jnp.zeros((blk, blk), BF16), bias_next,
                                jnp.zeros((blk, n_ctx), BF16)], axis=1)
        for h in range(ATT_KV_HEADS):
            for g in range(ATT_GROUPS):
                rows = slice(nq * g, nq * (g + 1))
                sg = s_buf[1 - slot, h, rows, :] + bias
                snk = jnp.full((1, 1), sink_ref[ATT_GROUPS * h + g] * LOG2E, F32)
                m = jnp.maximum(jnp.max(sg, axis=-1, keepdims=True), snk.astype(BF16))
                p = jnp.exp2(sg - m)
                den = jnp.sum(p.astype(F32), axis=-1, keepdims=True) + jnp.exp2(snk - m.astype(F32))
                p_buf[slot, h, rows, :] = p
                den_buf[slot, h, rows, :] = jnp.broadcast_to(den, (nq, w))
        for r, src in enumerate((kp_ref, kc_ref, kn_ref)):
            kall[blk * r:blk * (r + 1), :] = src[...]
        for h in range(ATT_KV_HEADS):
            qs = []
            for g in range(ATT_GROUPS):
                a = ATT_GROUPS * h + g
                piece = q_ref[:, w * (a // 2):w * (a // 2 + 1)]
                qs.append(jnp.where(low if a % 2 == 0 else jnp.logical_not(low), piece, jnp.zeros_like(piece)))
            s_buf[slot, h] = _dot_nt(jnp.concatenate(qs, axis=0), kall[:, w * h:w * (h + 1)]).astype(BF16)

    @pl.when(i % 2 == 0)
    def _():
        step(0)

    @pl.when(i % 2 == 1)
    def _():
        step(1)


def _ret_kernel(lg_ref, q_ref, k_ref, v_ref, g_ref, *rest, t_len, c_len, prompt):
    if prompt:
        gn_ref, o_ref, sfin_ref, obuf, sf, sb = rest
    else:
        s0_ref, gn_ref, o_ref, obuf, sf, sb = rest
    nc = t_len // c_len
    h = pl.program_id(1)
    lgf = lg_ref[0, h]
    lgb = lg_ref[1, h]
    row = lax.broadcasted_iota(jnp.int32, (c_len, c_len), 0).astype(F32)
    col = lax.broadcasted_iota(jnp.int32, (c_len, c_len), 1).astype(F32)
    diff = row - col
    decay_f = jnp.where(diff >= 0, jnp.exp(lgf * jnp.maximum(diff, 0.0)), 0.0)
    decay_b = jnp.where(diff < 0, jnp.exp(lgb * jnp.maximum(-diff, 0.0)), 0.0)
    idx = lax.broadcasted_iota(jnp.int32, (c_len, 1), 0).astype(F32)
    one = jnp.ones((1, 1), F32)
    wq_f, ws_f, gc_f = jnp.exp(lgf * (idx + 1.0)), jnp.exp(lgf * (c_len - 1.0 - idx)), jnp.exp(lgf * c_len * one)
    wq_b, ws_b, gc_b = jnp.exp(lgb * (c_len - idx)), jnp.exp(lgb * idx), jnp.exp(lgb * c_len * one)
    if prompt:
        sf[...] = jnp.zeros(sf.shape, F32)
        sb[...] = jnp.zeros(sb.shape, F32)
    else:
        sf[...] = s0_ref[0]
        sb[...] = s0_ref[1]

    def rows(c):
        start = c * c_len
        return pl.ds(start if isinstance(start, int) else pl.multiple_of(start, c_len), c_len)

    def chunk(sl, state, decay, wq, ws, gc):
        qc, kc, vc = q_ref[sl, :], k_ref[sl, :], v_ref[sl, :]
        s_prev = state[...]
        sc = (_dot_nt(qc, kc) * decay).astype(BF16)
        o = _dot(sc, vc) + _dot(qc, s_prev.astype(BF16)) * wq
        kw = (kc.astype(F32) * ws).astype(BF16)
        state[...] = gc * s_prev + _dot_tn(kw, vc)
        return o

    def fwd(sl):
        return chunk(sl, sf, decay_f, wq_f, ws_f, gc_f)

    def bwd(sl):
        return chunk(sl, sb, decay_b, wq_b, ws_b, gc_b)

    def finish(sl, o):
        mu = jnp.mean(o, axis=-1, keepdims=True)
        dev = o - mu
        var = jnp.mean(dev * dev, axis=-1, keepdims=True)
        on = dev * lax.rsqrt(var + RMS_EPS) * gn_ref[...]
        o_ref[sl, :] = (g_ref[sl, :].astype(F32) * on).astype(BF16)

    half = nc // 2

    def first_half(i, carry):
        sl_f, sl_b = rows(i), rows(nc - 1 - i)
        obuf[sl_f, :] = fwd(sl_f)
        obuf[sl_b, :] = bwd(sl_b)
        return carry

    lax.fori_loop(0, half, first_half, 0)
    if nc % 2:
        sl = rows(half)
        finish(sl, fwd(sl) + bwd(sl))

    def second_half(i, carry):
        sl_f, sl_b = rows(i), rows(nc - 1 - i)
        finish(sl_f, fwd(sl_f) + obuf[sl_f, :])
        finish(sl_b, bwd(sl_b) + obuf[sl_b, :])
        return carry

    lax.fori_loop(half + nc % 2, nc, second_half, 0)
    if prompt:
        sfin_ref[0] = sf[...]
        sfin_ref[1] = sb[...]


def _softplus(x):
    return jnp.maximum(x, 0.0) + jnp.log1p(jnp.exp(-jnp.abs(x)))


def _two_steps(h, a0, u0, a1, u1):
    return a0 * h + u0, (a1 * a0) * h + (a1 * u0 + u1)


def _lru_kernel(xr_ref, prev_ref, next_ref, act_ref, cw_ref, cb_ref, wg_ref, br_ref, bi_ref, lam_ref,
                h0_ref, y_ref, hfin_ref, hf3, xc_buf, a3, u3, hb3, hc, *, nc):
    tc, d_rnn = xr_ref.shape
    bw = d_rnn // LRU_BLOCKS
    groups = tc // V7X_SUBLANES
    s = pl.program_id(1)
    is_b = s >= nc
    c = jnp.where(is_b, 2 * nc - 1 - s, s)
    base = pl.multiple_of(c * tc, tc)
    g0 = c * groups

    @pl.when(jnp.logical_not(is_b))
    def _():
        x = xr_ref[...]
        rowi = lax.broadcasted_iota(jnp.int32, (tc, d_rnn), 0)
        p1 = jnp.where(c > 0, prev_ref[V7X_SUBLANES - 1:V7X_SUBLANES, :], 0.0)
        p2 = jnp.where(c > 0, prev_ref[V7X_SUBLANES - 2:V7X_SUBLANES - 1, :], 0.0)
        n1 = jnp.where(c < nc - 1, next_ref[0:1, :], 0.0)
        xm1 = jnp.where(rowi == 0, p1, pltpu.roll(x, 1, 0))
        xm2 = jnp.where(rowi == 0, p2, jnp.where(rowi == 1, p1, pltpu.roll(x, 2, 0)))
        xp1 = jnp.where(rowi == tc - 1, n1, pltpu.roll(x, tc - 1, 0))
        xc_buf[pl.ds(base, tc), :] = (cw_ref[0:1, :] * xm2 + cw_ref[1:2, :] * xm1 + cw_ref[2:3, :] * x
                                      + cw_ref[3:4, :] * xp1 + cb_ref[...])

    xc = xc_buf[pl.ds(base, tc), :]
    xcb = xc.astype(BF16)
    c_row = (-0.5 * LRU_C) * _softplus(-lam_ref[...])
    half_xc = 0.5 * xc
    for n in range(LRU_BLOCKS):
        cs = slice(bw * n, bw * (n + 1))
        z = _dot(xcb[:, cs], wg_ref[n])
        tr = jnp.tanh(z[:, :bw] + br_ref[:, cs])
        ti = jnp.tanh(z[:, bw:] + bi_ref[:, cs])
        log_a = c_row[:, cs] * (tr + 1.0)
        a = jnp.exp(log_a)
        u = jnp.sqrt(-jnp.tanh(log_a) * (a * a + 1.0)) * ((ti + 1.0) * half_xc[:, cs])
        a3[:, :, cs] = a.reshape(groups, V7X_SUBLANES, bw)
        u3[:, :, cs] = u.reshape(groups, V7X_SUBLANES, bw)


    @pl.when(s == 0)
    def _():
        hc[...] = h0_ref[0:1, :]

    @pl.when(s == nc)
    def _():
        hc[...] = h0_ref[1:2, :]

    @pl.when(jnp.logical_not(is_b))
    def _():
        def group(i, h):
            for r in range(0, V7X_SUBLANES, 2):
                h0, h = _two_steps(h, a3[i, pl.ds(r, 1), :], u3[i, pl.ds(r, 1), :],
                                   a3[i, pl.ds(r + 1, 1), :], u3[i, pl.ds(r + 1, 1), :])
                hf3[g0 + i, pl.ds(r, 1), :] = h0
                hf3[g0 + i, pl.ds(r + 1, 1), :] = h
            return h
        h = lax.fori_loop(0, groups, group, hc[...])
        hc[...] = h

        @pl.when(s == nc - 1)
        def _():
            hfin_ref[0:1, :] = h

    @pl.when(is_b)
    def _():
        def group(k, h):
            i = groups - 1 - k
            for r in range(V7X_SUBLANES - 1, 0, -2):
                h0, h = _two_steps(h, a3[i, pl.ds(r, 1), :], u3[i, pl.ds(r, 1), :],
                                   a3[i, pl.ds(r - 1, 1), :], u3[i, pl.ds(r - 1, 1), :])
                hb3[i, pl.ds(r, 1), :] = h0
                hb3[i, pl.ds(r - 1, 1), :] = h
            return h
        h = lax.fori_loop(0, groups, group, hc[...])
        hc[...] = h
        rec = (hf3[pl.ds(g0, groups)] + hb3[...]).reshape(tc, d_rnn)
        y_ref[...] = (act_ref[...] * rec).astype(BF16)

        @pl.when(s == 2 * nc - 1)
        def _():
            hfin_ref[1:2, :] = h


def _out_mlp_kernel(*refs, n_prompt, split_x, split_out):
    x, rest = _read_x(refs, n_prompt, split_x)
    yp_ref, ys_ref, mod_ref, gm_ref, wo_ref, wu_ref, wd_ref = rest[:7]
    d = x.shape[1]
    d_ff = wu_ref.shape[1]
    is_prompt = pl.program_id(0) < n_prompt
    y = jnp.where(is_prompt, yp_ref[...], ys_ref[...])
    x1 = x + mod_ref[:, 2 * d:3 * d] * _dot(y, wo_ref[...])
    hn = _modulate(x1, gm_ref[...], mod_ref[:, 3 * d:4 * d], mod_ref[:, 4 * d:5 * d]).astype(BF16)
    acc = jnp.zeros(x1.shape, F32)
    for c in range(d_ff // FF_CHUNK):
        hh = jnp.maximum(_dot(hn, wu_ref[:, FF_CHUNK * c:FF_CHUNK * (c + 1)]), 0.0)
        acc = acc + _dot((hh * hh).astype(BF16), wd_ref[FF_CHUNK * c:FF_CHUNK * (c + 1), :])
    out = x1 + mod_ref[:, 5 * d:6 * d] * acc
    if split_out:
        op_ref, os_ref = rest[7:]

        @pl.when(is_prompt)
        def _():
            op_ref[...] = out

        @pl.when(jnp.logical_not(is_prompt))
        def _():
            os_ref[...] = out
    else:
        rest[7][...] = out


class _Dims:
    def __init__(self, bp, lp, bs, ts, d):
        assert bp * lp == ts, "prompt tokens must fill exactly one segment"
        self.bp, self.lp, self.bs, self.ts, self.d = bp, lp, bs, ts, d
        self.nseg = 1 + bs
        self.n = self.nseg * ts
        self.tm = min(TOKEN_TILE, ts)
        assert ts % self.tm == 0 and ts % ATT_BLOCK == 0
        assert ts % RET_KERNEL_CHUNK == 0 and lp % min(RET_KERNEL_CHUNK, lp) == 0
        self.tps = ts // self.tm
        self.ntiles = self.n // self.tm


def _in_kernel(body, dm, split_x):
    return functools.partial(body, n_prompt=dm.tps, split_x=split_x)


def _attn_layer(dm, layer, slot, xs, norm_mix, mods, w_in, q_gain, k_gain, sink, cache_k, cache_v, rope):
    d, tm, tps, n = dm.d, dm.tm, dm.tps, dm.n
    split_x = len(xs) == 2
    hd = ATT_HEAD_DIM
    q_w, kv_w = ATT_HEADS * hd, ATT_KV_HEADS * hd
    gain = jnp.concatenate([jnp.tile(q_gain, ATT_HEADS), jnp.tile(k_gain, ATT_KV_HEADS)])[None, :]
    gw = q_w + kv_w
    assert (q_w + kv_w) % gw == 0 and gw % V7X_LANES == 0 and gw // hd <= V7X_LANES
    head_of_lane = np.arange(gw) // hd
    hsum = jnp.asarray(head_of_lane[:, None] == np.arange(V7X_LANES)[None, :], BF16)
    hexp = jnp.asarray(np.arange(V7X_LANES)[:, None] == head_of_lane[None, :], BF16)
    cos, sin = rope
    nt = dm.ntiles
    cur = lambda i: jnp.minimum(i, nt - 1)
    prev = lambda i: jnp.maximum(i - 1, 0)
    tok = lambda w: pl.BlockSpec((tm, w), lambda i: (prev(i), 0))
    q, k, v, kf, vf = pl.pallas_call(
        _in_kernel(_attn_in_kernel, dm, split_x),
        grid=(nt + 1,),
        in_specs=_token_specs(tm, tps, d, layer, split_x, cur) + [
            _resident(w_in.shape), _resident(gain.shape)] + _rope_specs(tm, tps, V7X_LANES, prev) + [
            _resident(hsum.shape), _resident(hexp.shape)],
        out_specs=[tok(q_w), tok(2 * kv_w), tok(2 * kv_w), tok(kv_w), tok(kv_w)],
        out_shape=[jax.ShapeDtypeStruct((n, q_w), BF16), jax.ShapeDtypeStruct((n, 2 * kv_w), BF16),
                   jax.ShapeDtypeStruct((n, 2 * kv_w), BF16), jax.ShapeDtypeStruct((n, kv_w), F32),
                   jax.ShapeDtypeStruct((n, kv_w), F32)],
        scratch_shapes=[pltpu.VMEM((2, tm, q_w + 2 * kv_w), F32)],
        compiler_params=_cparams(("arbitrary",)),
        name="attn_in",
    )(*xs, norm_mix, mods, w_in, gain, cos, sin, hsum, hexp)
    kf, vf = kf[:dm.ts], vf[:dm.ts]

    smem = pl.BlockSpec(memory_space=pltpu.SMEM)
    lp, bp, bs, ts = dm.lp, dm.bp, dm.bs, dm.ts
    y_p = pl.pallas_call(
        _attn_prompt_kernel,
        grid=(bp,),
        in_specs=[smem, pl.BlockSpec((lp, q_w), lambda b: (b, 0)),
                  pl.BlockSpec((lp, 2 * kv_w), lambda b: (b, 0)),
                  pl.BlockSpec((lp, 2 * kv_w), lambda b: (b, 0))],
        out_specs=pl.BlockSpec((lp, q_w), lambda b: (b, 0)),
        out_shape=jax.ShapeDtypeStruct((ts, q_w), BF16),
        compiler_params=_cparams(("arbitrary",)),
        name="attn_prompt",
    )(sink, q, k, v)

    blk = ATT_BLOCK
    nb = ts // blk
    nblocks = bs * nb
    past = cache_k.shape[2]
    nk = 3 * blk + past
    c1 = lambda i: jnp.minimum(i, nblocks - 1)
    c3 = lambda i: jnp.clip(i - 2, 0, nblocks - 1)
    cur = lambda c: lambda i: (nb + c(i), 0)
    prv = lambda c: lambda i: (nb + c(i) - jnp.where(c(i) % nb > 0, 1, 0), 0)
    nxt = lambda c: lambda i: (nb + c(i) + jnp.where(c(i) % nb < nb - 1, 1, 0), 0)
    kv_spec = lambda f: pl.BlockSpec((blk, 2 * kv_w), f)
    ctx_spec = lambda c: pl.BlockSpec((None, None, past, kv_w), lambda i: (c(i) // nb, slot, 0, 0))
    rows = ATT_GROUPS * blk
    y_s = pl.pallas_call(
        functools.partial(_attn_sample_kernel, nb=nb, nblocks=nblocks),
        grid=(nblocks + 2,),
        in_specs=[smem, pl.BlockSpec((blk, q_w), cur(c1)),
                  kv_spec(prv(c1)), kv_spec(cur(c1)), kv_spec(nxt(c1)),
                  kv_spec(prv(c3)), kv_spec(cur(c3)), kv_spec(nxt(c3)), ctx_spec(c1), ctx_spec(c3)],
        out_specs=pl.BlockSpec((blk, q_w), lambda i: (c3(i), 0)),
        out_shape=jax.ShapeDtypeStruct((bs * ts, q_w), BF16),
        scratch_shapes=[pltpu.VMEM((nk, 2 * kv_w), BF16), pltpu.VMEM((nk, 2 * kv_w), BF16),
                        pltpu.VMEM((2, ATT_KV_HEADS, rows, nk), BF16),
                        pltpu.VMEM((2, ATT_KV_HEADS, rows, nk), BF16),
                        pltpu.VMEM((2, ATT_KV_HEADS, rows, V7X_LANES), F32)],
        compiler_params=_cparams(("arbitrary",)),
        name="attn_sample",
    )(sink, q, k, k, k, v, v, v, cache_k, cache_v)
    return y_p, y_s, kf, vf


def _ret_layer(dm, layer, slot, xs, norm_mix, mods, w_in, gn_gain, log_decay, state, rope):
    d, tm, tps, n = dm.d, dm.tm, dm.tps, dm.n
    split_x = len(xs) == 2
    e1, ev = RET_HEADS * RET_DK, RET_HEADS * RET_DV
    cos, sin = rope
    tok = lambda w: pl.BlockSpec((tm, w), lambda i: (i, 0))
    q, k, v, g = pl.pallas_call(
        _in_kernel(_ret_in_kernel, dm, split_x),
        grid=(dm.ntiles,),
        in_specs=(_token_specs(tm, tps, d, layer, split_x) + [_resident(w_in.shape)]
                  + _rope_specs(tm, tps, RET_DK)),
        out_specs=[tok(e1), tok(e1), tok(ev), tok(ev)],
        out_shape=[jax.ShapeDtypeStruct((n, e1), BF16), jax.ShapeDtypeStruct((n, e1), BF16),
                   jax.ShapeDtypeStruct((n, ev), BF16), jax.ShapeDtypeStruct((n, ev), BF16)],
        compiler_params=_cparams(("arbitrary",)),
        name="ret_in",
    )(*xs, norm_mix, mods, w_in, cos, sin)

    smem = pl.BlockSpec(memory_space=pltpu.SMEM)
    gn = gn_gain[None, :]
    lp, bp, bs, ts = dm.lp, dm.bp, dm.bs, dm.ts
    state_spec = lambda f: pl.BlockSpec((None, None, 2, None, RET_DK, RET_DV), f)

    def call(t_len, nbatch, seg0, prompt, name):
        rows = lambda w: pl.BlockSpec((t_len, w), lambda b, h: (seg0 + b, h))
        in_specs = [smem, rows(RET_DK), rows(RET_DK), rows(RET_DV), rows(RET_DV)]
        args = [log_decay, q, k, v, g]
        out_shape = [jax.ShapeDtypeStruct((nbatch * t_len, ev), BF16)]
        out_specs = [pl.BlockSpec((t_len, RET_DV), lambda b, h: (b, h))]
        if prompt:
            out_shape.append(jax.ShapeDtypeStruct((nbatch, 1, 2, RET_HEADS, RET_DK, RET_DV), F32))
            out_specs.append(state_spec(lambda b, h: (b, 0, 0, h, 0, 0)))
        else:
            in_specs.append(state_spec(lambda b, h: (b, slot, 0, h, 0, 0)))
            args.append(state)
        in_specs.append(pl.BlockSpec((1, RET_DV), lambda b, h: (0, h)))
        args.append(gn)
        return pl.pallas_call(
            functools.partial(_ret_kernel, t_len=t_len, c_len=min(RET_KERNEL_CHUNK, t_len), prompt=prompt),
            grid=(nbatch, RET_HEADS),
            in_specs=in_specs, out_specs=out_specs, out_shape=out_shape,
            scratch_shapes=[pltpu.VMEM((t_len, RET_DV), F32), pltpu.VMEM((RET_DK, RET_DV), F32),
                            pltpu.VMEM((RET_DK, RET_DV), F32)],
            compiler_params=_cparams(("arbitrary", "arbitrary")),
            name=name,
        )(*args)

    y_p, s_new = call(lp, bp, 0, True, "ret_prompt")
    (y_s,) = call(ts, bs, 1, False, "ret_sample")
    return y_p, y_s, s_new


def _lru_layer(dm, layer, slot, xs, norm_mix, mods, w_in, conv_w, conv_b, w_r, b_r, w_i, b_i, lam, state):
    d, tm, tps, n = dm.d, dm.tm, dm.tps, dm.n
    split_x = len(xs) == 2
    d_rnn = w_in.shape[1] // 2
    tok = lambda w: pl.BlockSpec((tm, w), lambda i: (i, 0))
    act, xr = pl.pallas_call(
        _in_kernel(_lru_in_kernel, dm, split_x),
        grid=(dm.ntiles,),
        in_specs=_token_specs(tm, tps, d, layer, split_x) + [_resident(w_in.shape)],
        out_specs=[tok(d_rnn), tok(d_rnn)],
        out_shape=[jax.ShapeDtypeStruct((n, d_rnn), F32), jax.ShapeDtypeStruct((n, d_rnn), F32)],
        compiler_params=_cparams(("arbitrary",)),
        name="lru_in",
    )(*xs, norm_mix, mods, w_in)

    wg = (0.5 * jnp.concatenate([w_r, w_i], axis=-1)).astype(BF16)
    b_r, b_i = 0.5 * b_r, 0.5 * b_i
    vec = lambda a: a[:, None, :]
    lp, bp, bs, ts = dm.lp, dm.bp, dm.bs, dm.ts
    sub = V7X_SUBLANES

    def call(t_len, nbatch, row0, h0, h0_slot, name):
        tc = t_len if t_len <= 256 else min(512, t_len // 2)
        nc = t_len // tc
        cb = row0 // tc
        early = lambda s: jnp.minimum(s, nc - 1)
        late = lambda s: jnp.where(s >= nc, 2 * nc - 1 - s, nc - 1)
        dirv = lambda s: jnp.where(s >= nc, 1, 0)
        r8 = tc // sub
        first8 = lambda b: (row0 + b * t_len) // sub
        prev8 = lambda b, s: (jnp.maximum(first8(b) + early(s) * r8 - 1, first8(b)), 0)
        next8 = lambda b, s: (jnp.minimum(first8(b) + (early(s) + 1) * r8, first8(b) + t_len // sub - 1), 0)
        dspec = lambda shape: pl.BlockSpec((None,) + shape, lambda b, s: (dirv(s),) + (0,) * len(shape))
        return pl.pallas_call(
            functools.partial(_lru_kernel, nc=nc),
            grid=(nbatch, 2 * nc),
            in_specs=[pl.BlockSpec((tc, d_rnn), lambda b, s: (cb + b * nc + early(s), 0)),
                      pl.BlockSpec((sub, d_rnn), prev8), pl.BlockSpec((sub, d_rnn), next8),
                      pl.BlockSpec((tc, d_rnn), lambda b, s: (cb + b * nc + late(s), 0)),
                      pl.BlockSpec((CONV_W, d_rnn), lambda b, s: (0, 0)),
                      pl.BlockSpec((1, d_rnn), lambda b, s: (0, 0)),
                      dspec(wg.shape[1:]), dspec((1, d_rnn)), dspec((1, d_rnn)), dspec((1, d_rnn)),
                      pl.BlockSpec((None, None, 2, d_rnn), lambda b, s: (b, h0_slot, 0, 0))],
            out_specs=[pl.BlockSpec((tc, d_rnn), lambda b, s: (b * nc + late(s), 0)),
                       pl.BlockSpec((None, 2, d_rnn), lambda b, s: (b, 0, 0))],
            out_shape=[jax.ShapeDtypeStruct((nbatch * t_len, d_rnn), BF16),
                       jax.ShapeDtypeStruct((nbatch, 2, d_rnn), F32)],
            scratch_shapes=[pltpu.VMEM((t_len // sub, sub, d_rnn), F32), pltpu.VMEM((t_len, d_rnn), F32),
                            pltpu.VMEM((tc // sub, sub, d_rnn), F32), pltpu.VMEM((tc // sub, sub, d_rnn), F32),
                            pltpu.VMEM((tc // sub, sub, d_rnn), F32), pltpu.VMEM((1, d_rnn), F32)],
            compiler_params=_cparams(("arbitrary", "arbitrary")),
            name=name,
        )(xr, xr, xr, act, conv_w, conv_b[None, :], wg, vec(b_r), vec(b_i), vec(lam), h0)

    y_p, h_new = call(lp, bp, 0, jnp.zeros((bp, 1, 2, d_rnn), F32), 0, "lru_prompt")
    y_s, _ = call(ts, bs, ts, state, slot, "lru_sample")
    return y_p, y_s, h_new


def _out_mlp(dm, layer, xs, y_p, y_s, norm_mlp, mods, w_out, w_up, w_down, split_out):
    d, tm, tps, n = dm.d, dm.tm, dm.tps, dm.n
    split_x = len(xs) == 2
    din = w_out.shape[0]
    first = lambda i: (jnp.minimum(i, tps - 1), 0)
    second = lambda i: (jnp.maximum(i - tps, 0), 0)
    if split_out:
        out_specs = [pl.BlockSpec((tm, d), first), pl.BlockSpec((tm, d), second)]
        out_shape = [jax.ShapeDtypeStruct((dm.ts, d), F32), jax.ShapeDtypeStruct((n - dm.ts, d), F32)]
    else:
        out_specs = pl.BlockSpec((tm, d), lambda i: (i, 0))
        out_shape = jax.ShapeDtypeStruct((n, d), F32)
    return pl.pallas_call(
        functools.partial(_out_mlp_kernel, n_prompt=tps, split_x=split_x, split_out=split_out),
        grid=(dm.ntiles,),
        in_specs=_x_specs(tm, tps, d, split_x) + [
            pl.BlockSpec((tm, din), first), pl.BlockSpec((tm, din), second),
            pl.BlockSpec((None, None, 1, N_MOD * d), lambda i: (layer, i // tps, 0, 0)),
            pl.BlockSpec((None, 1, d), lambda i: (layer, 0, 0)),
            _resident(w_out.shape), _layer_resident(w_up.shape, layer), _layer_resident(w_down.shape, layer)],
        out_specs=out_specs, out_shape=out_shape,
        compiler_params=_cparams(("arbitrary",)),
        name="out_mlp",
    )(*xs, y_p, y_s, mods, norm_mlp, w_out, w_up, w_down)


def kernel(x_prompt, x_sample, cache_attn_k, cache_attn_v, state_ret, state_lru, c, c_ctx, norm_mix, norm_mlp, w_ada, b_ada, w_up, w_down, attn_w_in, attn_w_out, attn_q_gain, attn_k_gain, attn_sink, ret_w_in, ret_w_out, ret_gn_gain, ret_log_decay, lru_w_in, lru_conv_w, lru_conv_b, lru_w_r, lru_b_r, lru_w_i, lru_b_i, lru_lambda, lru_w_out):
    bp, lp, d = x_prompt.shape
    bs, ts, _ = x_sample.shape
    dm = _Dims(bp, lp, bs, ts, d)
    depth = w_ada.shape[0]
    assert dm.nseg <= MOD_ROWS

    xs = (x_prompt.reshape(ts, d), x_sample.reshape(bs * ts, d))
    cvec = jnp.concatenate([c_ctx[None, :], c, jnp.zeros((MOD_ROWS - dm.nseg, d), F32)], axis=0)
    mods = _modulation(cvec, w_ada, b_ada)[:, :dm.nseg].reshape(depth, dm.nseg, 1, N_MOD * d)
    nmix = norm_mix[:, None, :]
    nmlp = norm_mlp[:, None, :]

    rope_attn = tuple(jnp.tile(t, (1, 1, 2)) for t in _rope_tables(ts, ATT_HEAD_DIM // 2))
    rope_ret = _rope_tables(ts, RET_DK // 2)
    kv_w = ATT_KV_HEADS * ATT_HEAD_DIM
    past = cache_attn_k.shape[2]
    ck = cache_attn_k.reshape(bs, -1, past, kv_w)
    cv = cache_attn_v.reshape(bs, -1, past, kv_w)

    w_up_b, w_down_b = w_up.astype(BF16), w_down.astype(BF16)

    new_k, new_v, new_ret, new_lru = [], [], [], []
    for layer in range(depth):
        kind, slot = layer % 3, layer // 3
        if kind == 0:
            y_p, y_s, kf, vf = _attn_layer(dm, layer, slot, xs, nmix, mods, attn_w_in[slot].astype(BF16),
                                           attn_q_gain[slot], attn_k_gain[slot], attn_sink[slot],
                                           ck, cv, rope_attn)
            new_k.append(kf.reshape(bp, lp, ATT_KV_HEADS, ATT_HEAD_DIM))
            new_v.append(vf.reshape(bp, lp, ATT_KV_HEADS, ATT_HEAD_DIM))
            w_out = attn_w_out[slot]
        elif kind == 1:
            y_p, y_s, s_new = _ret_layer(dm, layer, slot, xs, nmix, mods, ret_w_in[slot].astype(BF16),
                                         ret_gn_gain[slot], ret_log_decay[slot], state_ret, rope_ret)
            new_ret.append(s_new)
            w_out = ret_w_out[slot]
        else:
            y_p, y_s, h_new = _lru_layer(dm, layer, slot, xs, nmix, mods, lru_w_in[slot].astype(BF16),
                                         lru_conv_w[slot], lru_conv_b[slot], lru_w_r[slot], lru_b_r[slot],
                                         lru_w_i[slot], lru_b_i[slot], lru_lambda[slot], state_lru)
            new_lru.append(h_new)
            w_out = lru_w_out[slot]
        last = layer == depth - 1
        out = _out_mlp(dm, layer, xs, y_p, y_s, nmlp, mods, w_out.astype(BF16), w_up_b, w_down_b,
                       split_out=last)
        xs = tuple(out) if last else (out,)

    y_prompt = xs[0].reshape(bp, lp, d)
    y_sample = xs[1].reshape(bs, ts, d)
    return (y_prompt, y_sample, jnp.stack(new_k, axis=1), jnp.stack(new_v, axis=1),
            jnp.concatenate(new_ret, axis=1), jnp.stack(new_lru, axis=1))
```
